```python
import math
import jax
import jax.numpy as jnp
from jax import lax
import numpy as np

D_MODEL = 1024
BATCH = 8
SEQ = 2048
DEPTH = 1
DEC_BATCH = 128
DEC_SEQ = 1
PAST_LEN = 2048
PAGE_SIZE = 128

HEAD_DIM = 64
N_HEADS = D_MODEL // HEAD_DIM
ATTN_WIDTH = N_HEADS * HEAD_DIM
ROT_DIM = HEAD_DIM // 4
ROPE_THETA = 500000.0
MOBA_BLOCK = 256
MOBA_TOPK = 3
Q_CHUNK = 64
SSM_HEAD_DIM = 64
SSM_WIDTH = D_MODEL
SSM_HEADS = SSM_WIDTH // SSM_HEAD_DIM
SSM_GROUPS = 2
D_STATE = 128
CONV_W = 4
SSD_CHUNK = 128
CONV_DIM = SSM_WIDTH + 2 * SSM_GROUPS * D_STATE
MIX_WIDTH = ATTN_WIDTH + SSM_WIDTH
IN_COLS = 3 * ATTN_WIDTH + SSM_WIDTH + CONV_DIM + SSM_HEADS
PEER_HEADS = 8
PEER_NKEYS = 128
PEER_EXPERTS = PEER_NKEYS * PEER_NKEYS
PEER_QDIM = 256
PEER_TOPK = 16
PEER_CHUNK = 256
EPS = 1e-6
NEG_INF = -1e30

kernel_name = 'hybrid_moba_ssd_peer_step'


def _rms(x):
    xf = x.astype(jnp.float32)
    y = xf * lax.rsqrt(jnp.mean(xf * xf, axis=-1, keepdims=True) + EPS)
    return y.astype(x.dtype)


def rmsnorm(x, g):
    return _rms(x) * g


def apply_partial_rope(x, pos):
    half = ROT_DIM // 2
    inv_freq = ROPE_THETA ** (-jnp.arange(half, dtype=jnp.float32) / half)
    ang = pos.astype(jnp.float32)[:, None] * inv_freq[None, :]
    cos = jnp.cos(ang)[:, None, :]
    sin = jnp.sin(ang)[:, None, :]
    xr = x[..., :ROT_DIM].astype(jnp.float32)
    x1, x2 = xr[..., :half], xr[..., half:]
    rot = jnp.concatenate([x1 * cos - x2 * sin, x2 * cos + x1 * sin], axis=-1).astype(x.dtype)
    return jnp.concatenate([rot, x[..., ROT_DIM:]], axis=-1)


def dense_fetch(rows):
    n = rows.shape[0]
    def fetch(pos):
        h = jnp.arange(N_HEADS).reshape((N_HEADS,) + (1,) * (pos.ndim - 1))
        return rows[jnp.clip(pos, 0, n - 1), h]
    return fetch


def paged_fetch(pool, new, pt_row, past_len):
    n_pages = pt_row.shape[0]
    t_new = new.shape[0]
    def fetch(pos):
        h = jnp.arange(N_HEADS).reshape((N_HEADS,) + (1,) * (pos.ndim - 1))
        phys = pt_row[jnp.clip(pos // PAGE_SIZE, 0, n_pages - 1)]
        r_past = pool[phys, pos % PAGE_SIZE, h]
        r_new = new[jnp.clip(pos - past_len, 0, t_new - 1), h]
        return jnp.where((pos < past_len)[..., None], r_past, r_new)
    return fetch


def moba_core(q, q_pos, means, fetch_k, fetch_v):
    nh, nq, _ = q.shape
    nb = means.shape[1]
    ksel = min(MOBA_TOPK, nb)
    q_blk = q_pos // MOBA_BLOCK
    gate = jnp.einsum('hqd,hnd->hqn', q, means).astype(jnp.float32)
    gate = jnp.where(jnp.arange(nb)[None, :] < q_blk[:, None], gate, NEG_INF)
    _, idx = lax.top_k(gate, ksel)
    sel_ok = idx < q_blk[:, None]
    offs = jnp.arange(MOBA_BLOCK, dtype=jnp.int32)
    sel_pos = idx[..., None] * MOBA_BLOCK + offs
    own_pos = jnp.broadcast_to((q_blk * MOBA_BLOCK)[:, None] + offs, (nh, nq, MOBA_BLOCK))
    k_sel, v_sel = fetch_k(sel_pos), fetch_v(sel_pos)
    k_own, v_own = fetch_k(own_pos), fetch_v(own_pos)
    scale = HEAD_DIM ** -0.5
    s_sel = jnp.einsum('hqd,hqjtd->hqjt', q, k_sel).astype(jnp.float32) * scale
    s_sel = jnp.where(sel_ok[..., None], s_sel, NEG_INF).reshape(nh, nq, ksel * MOBA_BLOCK)
    s_own = jnp.einsum('hqd,hqtd->hqt', q, k_own).astype(jnp.float32) * scale
    s_own = jnp.where(own_pos <= q_pos[:, None], s_own, NEG_INF)
    p = jax.nn.softmax(jnp.concatenate([s_sel, s_own], axis=-1), axis=-1).astype(v_own.dtype)
    p_sel = p[..., :ksel * MOBA_BLOCK].reshape(nh, nq, ksel, MOBA_BLOCK)
    p_own = p[..., ksel * MOBA_BLOCK:]
    return (jnp.einsum('hqjt,hqjtd->hqd', p_sel, v_sel)
            + jnp.einsum('hqt,hqtd->hqd', p_own, v_own))


def moba_prompt(q, k, v):
    bn, s = q.shape[:2]
    nb = -(-s // MOBA_BLOCK)
    kp = jnp.pad(k.astype(jnp.float32), ((0, 0), (0, nb * MOBA_BLOCK - s), (0, 0), (0, 0)))
    means = kp.reshape(bn, nb, MOBA_BLOCK, N_HEADS, HEAD_DIM).mean(axis=2)
    means = means.transpose(0, 2, 1, 3).astype(q.dtype)
    nq = s // Q_CHUNK
    pc = jnp.arange(s, dtype=jnp.int32).reshape(nq, Q_CHUNK)

    def one_seq(args):
        q1, k1, v1, m1 = args
        fk, fv = dense_fetch(k1), dense_fetch(v1)
        qc = q1.reshape(nq, Q_CHUNK, N_HEADS, HEAD_DIM).transpose(0, 2, 1, 3)
        o = lax.map(lambda a: moba_core(a[0], a[1], m1, fk, fv), (qc, pc))
        return o.transpose(0, 2, 1, 3).reshape(s, ATTN_WIDTH)

    return lax.map(one_seq, (q, k, v, means))


def moba_sample(q, k, v, pool_k, pool_v, page_table):
    bn, t = q.shape[:2]
    n_pages = page_table.shape[1]
    past = n_pages * PAGE_SIZE
    nb = -(-(past + t) // MOBA_BLOCK)
    ppb = MOBA_BLOCK // PAGE_SIZE
    n_new_pages = -(-t // PAGE_SIZE)
    page_sums = pool_k.astype(jnp.float32).sum(axis=1)[page_table]
    kn = jnp.pad(k.astype(jnp.float32), ((0, 0), (0, n_new_pages * PAGE_SIZE - t), (0, 0), (0, 0)))
    new_sums = kn.reshape(bn, n_new_pages, PAGE_SIZE, N_HEADS, HEAD_DIM).sum(axis=2)
    sums = jnp.concatenate([page_sums, new_sums], axis=1)
    sums = jnp.pad(sums, ((0, 0), (0, nb * ppb - sums.shape[1]), (0, 0), (0, 0)))
    means = sums.reshape(bn, nb, ppb, N_HEADS, HEAD_DIM).sum(axis=2) / MOBA_BLOCK
    means = means.transpose(0, 2, 1, 3).astype(q.dtype)
    q_pos = past + jnp.arange(t, dtype=jnp.int32)

    def one_seq(args):
        q1, k1, v1, m1, pt = args
        fk = paged_fetch(pool_k, k1, pt, past)
        fv = paged_fetch(pool_v, v1, pt, past)
        o = moba_core(q1.transpose(1, 0, 2), q_pos, m1, fk, fv)
        return o.transpose(1, 0, 2).reshape(t, ATTN_WIDTH)

    return lax.map(one_seq, (q, k, v, means, page_table))


def causal_conv(xbc, conv_state, w, b):
    xp = jnp.concatenate([conv_state.astype(xbc.dtype), xbc], axis=1)
    y = lax.conv_general_dilated(xp, w[:, None, :].astype(xbc.dtype), window_strides=(1,),
                                 padding='VALID', dimension_numbers=('NWC', 'WIO', 'NWC'),
                                 feature_group_count=CONV_DIM)
    return jax.nn.silu(y + b), xp[:, xp.shape[1] - (CONV_W - 1):]


def ssd_scan(x, dt, a, b_in, c_in, h0):
    bn, l = x.shape[:2]
    q = min(SSD_CHUNK, l)
    nc = -(-l // q)
    pad = nc * q - l
    hg = SSM_HEADS // SSM_GROUPS

    def blocks(t, tail):
        t = jnp.pad(t, ((0, 0), (0, pad)) + ((0, 0),) * (t.ndim - 2))
        return jnp.moveaxis(t.reshape((bn, nc, q) + tail), 1, 0)

    xc = blocks(x.astype(jnp.float32), (SSM_GROUPS, hg, SSM_HEAD_DIM))
    dtc = blocks(dt, (SSM_GROUPS, hg))
    bc = blocks(b_in.astype(jnp.float32), (SSM_GROUPS, D_STATE))
    cc = blocks(c_in.astype(jnp.float32), (SSM_GROUPS, D_STATE))
    ag = a.reshape(SSM_GROUPS, hg)
    causal = jnp.tril(jnp.ones((q, q), dtype=bool))[None, :, :, None, None]

    def step(h, inp):
        xq, dq, bq, cq = inp
        acs = jnp.cumsum(dq * ag, axis=1)
        seg = acs[:, :, None] - acs[:, None, :]
        decay = jnp.exp(jnp.where(causal, seg, -jnp.inf))
        cb = jnp.einsum('btgn,bsgn->btsg', cq, bq)
        w = cb[..., None] * decay * dq[:, None]
        y = jnp.einsum('btsgh,bsghp->btghp', w, xq)
        y = y + jnp.einsum('btgn,bghpn->btghp', cq, h) * jnp.exp(acs)[..., None]
        to_end = jnp.exp(acs[:, -1:] - acs) * dq
        h = (h * jnp.exp(acs[:, -1])[..., None, None]
             + jnp.einsum('bsgn,bsgh,bsghp->bghpn', bq, to_end, xq))
        return h, y

    h0g = h0.astype(jnp.float32).reshape(bn, SSM_GROUPS, hg, SSM_HEAD_DIM, D_STATE)
    h_t, y = lax.scan(step, h0g, (xc, dtc, bc, cc))
    y = jnp.moveaxis(y, 0, 1).reshape(bn, nc * q, SSM_HEADS, SSM_HEAD_DIM)[:, :l]
    return y, h_t.reshape(bn, SSM_HEADS, SSM_HEAD_DIM, D_STATE)


def token_mix(xn, pos, attend, conv_state, ssm_state, w_in, conv_w, conv_b, dt_bias, a_log,
              d_skip, g_attn, g_ssd, w_out):
    bn, l, _ = xn.shape
    proj = xn @ w_in
    q, k, v, z, xbc, dt_raw = jnp.split(
        proj, [ATTN_WIDTH, 2 * ATTN_WIDTH, 3 * ATTN_WIDTH, 3 * ATTN_WIDTH + SSM_WIDTH,
               3 * ATTN_WIDTH + SSM_WIDTH + CONV_DIM], axis=-1)
    heads = (bn, l, N_HEADS, HEAD_DIM)
    q = apply_partial_rope(q.reshape(heads), pos)
    k = apply_partial_rope(k.reshape(heads), pos)
    v = v.reshape(heads)
    attn = rmsnorm(attend(q, k, v), g_attn)
    xbc, conv_new = causal_conv(xbc, conv_state, conv_w, conv_b)
    xs, b_in, c_in = jnp.split(xbc, [SSM_WIDTH, SSM_WIDTH + SSM_GROUPS * D_STATE], axis=-1)
    xs = xs.reshape(bn, l, SSM_HEADS, SSM_HEAD_DIM)
    dt = jax.nn.softplus(dt_raw.astype(jnp.float32) + dt_bias.astype(jnp.float32))
    a = -jnp.exp(a_log.astype(jnp.float32))
    y, ssm_new = ssd_scan(xs, dt, a, b_in.reshape(bn, l, SSM_GROUPS, D_STATE),
                          c_in.reshape(bn, l, SSM_GROUPS, D_STATE), ssm_state)
    y = (y + d_skip.astype(jnp.float32)[:, None] * xs.astype(jnp.float32))
    y = y.reshape(bn, l, SSM_WIDTH).astype(xn.dtype) * jax.nn.silu(z)
    y = _rms(y.reshape(bn, l, SSM_GROUPS, SSM_WIDTH // SSM_GROUPS)).reshape(bn, l, SSM_WIDTH) * g_ssd
    out = jnp.concatenate([attn, y], axis=-1) @ w_out
    return out, k, v, conv_new, ssm_new.astype(xn.dtype)


def peer_ffn(xn, wq, sub_keys, u_emb, v_emb):
    bn, l, d = xn.shape
    t = bn * l
    c = min(PEER_CHUNK, t)
    nc = -(-t // c)
    x2 = jnp.pad(xn.reshape(t, d), ((0, nc * c - t), (0, 0))).reshape(nc, c, d)

    def chunk(xb):
        qh = (xb @ wq).reshape(c, PEER_HEADS, 2, PEER_QDIM // 2)
        s = jnp.einsum('chid,hikd->chik', qh, sub_keys).astype(jnp.float32)
        sv, si = lax.top_k(s, PEER_TOPK)
        cand = (sv[:, :, 0, :, None] + sv[:, :, 1, None, :]).reshape(c, PEER_HEADS, PEER_TOPK * PEER_TOPK)
        best, bi = lax.top_k(cand, PEER_TOPK)
        i1 = jnp.take_along_axis(si[:, :, 0], bi // PEER_TOPK, axis=-1)
        i2 = jnp.take_along_axis(si[:, :, 1], bi % PEER_TOPK, axis=-1)
        e = i1 * PEER_NKEYS + i2
        g = jax.nn.softmax(best, axis=-1).astype(xb.dtype)
        act = jax.nn.gelu(jnp.einsum('cd,chkd->chk', xb, u_emb[e]), approximate=False)
        return jnp.einsum('chk,chkd->cd', g * act, v_emb[e])

    out = lax.map(chunk, x2).reshape(nc * c, d)[:t]
    return out.reshape(bn, l, d)


def setup_inputs(seed: int = 0) -> dict:
    key = jax.random.key(seed)
    ks = jax.random.split(key, 24)
    f32 = jnp.float32

    def nrm(k, shape, s):
        return jax.random.normal(k, shape, f32) * s

    n_pages = PAST_LEN // PAGE_SIZE
    n_used = DEC_BATCH * n_pages
    n_pool = n_used + max(1, n_used // 4)
    x_prompt = nrm(ks[0], (BATCH, SEQ, D_MODEL), 1.0)
    x_sample = nrm(ks[1], (DEC_BATCH, DEC_SEQ, D_MODEL), 1.0)
    cache_k = nrm(ks[2], (DEPTH, n_pool, PAGE_SIZE, N_HEADS, HEAD_DIM), 1.0)
    cache_v = nrm(ks[3], (DEPTH, n_pool, PAGE_SIZE, N_HEADS, HEAD_DIM), 1.0)
    perm = jax.random.permutation(ks[4], n_pool)
    page_table = perm[:n_used].reshape(DEC_BATCH, n_pages).astype(jnp.int32)
    state_conv = nrm(ks[5], (DEPTH, DEC_BATCH, CONV_W - 1, CONV_DIM), 1.0)
    state_ssm = nrm(ks[6], (DEPTH, DEC_BATCH, SSM_HEADS, SSM_HEAD_DIM, D_STATE), 0.1)
    g_mix = 1.0 + nrm(ks[7], (DEPTH, D_MODEL), 0.02)
    w_in = nrm(ks[8], (DEPTH, D_MODEL, IN_COLS), D_MODEL ** -0.5)
    conv_w = nrm(ks[9], (DEPTH, CONV_W, CONV_DIM), CONV_W ** -0.5)
    conv_b = nrm(ks[10], (DEPTH, CONV_DIM), 0.02)
    dt0 = jnp.exp(jax.random.uniform(ks[11], (DEPTH, SSM_HEADS), f32, math.log(1e-3), math.log(1e-1)))
    dt_bias = dt0 + jnp.log(-jnp.expm1(-dt0))
    a_log = jnp.log(jax.random.uniform(ks[12], (DEPTH, SSM_HEADS), f32, 1.0, 16.0))
    d_skip = 1.0 + nrm(ks[13], (DEPTH, SSM_HEADS), 0.1)
    g_attn = 1.0 + nrm(ks[14], (DEPTH, ATTN_WIDTH), 0.02)
    g_ssd = 1.0 + nrm(ks[15], (DEPTH, SSM_WIDTH), 0.02)
    w_out = nrm(ks[16], (DEPTH, MIX_WIDTH, D_MODEL), MIX_WIDTH ** -0.5)
    g_ffn = 1.0 + nrm(ks[17], (DEPTH, D_MODEL), 0.02)
    peer_wq = nrm(ks[18], (DEPTH, D_MODEL, PEER_HEADS * PEER_QDIM), D_MODEL ** -0.5)
    peer_keys = nrm(ks[19], (DEPTH, PEER_HEADS, 2, PEER_NKEYS, PEER_QDIM // 2), (PEER_QDIM // 2) ** -0.5)
    peer_u = nrm(ks[20], (DEPTH, PEER_EXPERTS, D_MODEL), D_MODEL ** -0.5)
    peer_v = nrm(ks[21], (DEPTH, PEER_EXPERTS, D_MODEL), PEER_HEADS ** -0.5)
    g_final = 1.0 + nrm(ks[22], (D_MODEL,), 0.02)
    return {'x_prompt': x_prompt, 'x_sample': x_sample, 'cache_k': cache_k, 'cache_v': cache_v,
            'page_table': page_table, 'state_conv': state_conv, 'state_ssm': state_ssm,
            'g_mix': g_mix, 'w_in': w_in, 'conv_w': conv_w, 'conv_b': conv_b, 'dt_bias': dt_bias,
            'a_log': a_log, 'd_skip': d_skip, 'g_attn': g_attn, 'g_ssd': g_ssd, 'w_out': w_out,
            'g_ffn': g_ffn, 'peer_wq': peer_wq, 'peer_keys': peer_keys, 'peer_u': peer_u,
            'peer_v': peer_v, 'g_final': g_final}


def reference(x_prompt, x_sample, cache_k, cache_v, page_table, state_conv, state_ssm,
              g_mix, w_in, conv_w, conv_b, dt_bias, a_log, d_skip, g_attn, g_ssd, w_out,
              g_ffn, peer_wq, peer_keys, peer_u, peer_v, g_final):
    pos_p = jnp.arange(x_prompt.shape[1], dtype=jnp.int32)
    pos_s = page_table.shape[1] * PAGE_SIZE + jnp.arange(x_sample.shape[1], dtype=jnp.int32)
    hp, hs = x_prompt, x_sample
    kp_l, vp_l, ks_l, vs_l, cp_l, cs_l, sp_l, ss_l = [], [], [], [], [], [], [], []
    for l in range(DEPTH):
        lw = (w_in[l], conv_w[l], conv_b[l], dt_bias[l], a_log[l], d_skip[l], g_attn[l], g_ssd[l], w_out[l])
        conv0 = jnp.zeros((hp.shape[0], CONV_W - 1, CONV_DIM), hp.dtype)
        ssm0 = jnp.zeros((hp.shape[0], SSM_HEADS, SSM_HEAD_DIM, D_STATE), hp.dtype)
        mp, kp, vp, cp, sp = token_mix(rmsnorm(hp, g_mix[l]), pos_p, moba_prompt, conv0, ssm0, *lw)
        attend_s = lambda q, k, v, l=l: moba_sample(q, k, v, cache_k[l], cache_v[l], page_table)
        ms, kk, vv, cs, ss = token_mix(rmsnorm(hs, g_mix[l]), pos_s, attend_s,
                                       state_conv[l], state_ssm[l], *lw)
        hp = hp + mp
        hs = hs + ms
        hp = hp + peer_ffn(rmsnorm(hp, g_ffn[l]), peer_wq[l], peer_keys[l], peer_u[l], peer_v[l])
        hs = hs + peer_ffn(rmsnorm(hs, g_ffn[l]), peer_wq[l], peer_keys[l], peer_u[l], peer_v[l])
        kp_l.append(kp)
        vp_l.append(vp)
        ks_l.append(kk)
        vs_l.append(vv)
        cp_l.append(cp)
        cs_l.append(cs)
        sp_l.append(sp)
        ss_l.append(ss)
    y_prompt = rmsnorm(hp, g_final)
    y_sample = rmsnorm(hs, g_final)
    k_prompt = jnp.stack(kp_l)
    v_prompt = jnp.stack(vp_l)
    k_sample = jnp.stack(ks_l)
    v_sample = jnp.stack(vs_l)
    conv_prompt = jnp.stack(cp_l)
    conv_sample = jnp.stack(cs_l)
    ssm_prompt = jnp.stack(sp_l)
    ssm_sample = jnp.stack(ss_l)
    return (y_prompt, y_sample, k_prompt, v_prompt, k_sample, v_sample, conv_prompt, conv_sample, ssm_prompt, ssm_sample)
```

```python
import functools
import math

import jax
import jax.numpy as jnp
from jax import lax
from jax.experimental import pallas as pl
from jax.experimental.pallas import tpu as pltpu

F32 = jnp.float32
BF16 = jnp.bfloat16

D_MODEL = 1024
HEAD_DIM = 64
N_HEADS = 16
ROT_DIM = 16
ROPE_THETA = 500000.0
MOBA_BLOCK = 256
MOBA_TOPK = 3
PAGE_SIZE = 128
SSM_HEAD_DIM = 64
SSM_HEADS = 16
SSM_GROUPS = 2
D_STATE = 128
CONV_W = 4
SSD_CHUNK = 128
SSM_WIDTH = 1024
CONV_DIM = SSM_WIDTH + 2 * SSM_GROUPS * D_STATE
PEER_HEADS = 8
PEER_NKEYS = 128
PEER_TOPK = 16
EPS = 1e-6
NEG_INF = -1e30

LANES = 128
SUBLANES = 8
VMEM_LIMIT = 52 * 1024 * 1024
ATTN_TQ = 128
SAMPLE_BLOCKS_PER_STEP = 4
PEER_TT = 512
PEER_IPS = SUBLANES
PEER_EC = PEER_IPS * PEER_NKEYS


def _nt(a, b):
    return lax.dot_general(a, b, (((1,), (1,)), ((), ())), preferred_element_type=F32)


def _tn(a, b):
    return lax.dot_general(a, b, (((0,), (0,)), ((), ())), preferred_element_type=F32)


def _nn(a, b):
    return jnp.dot(a, b, preferred_element_type=F32)


def _split3(x):
    hi = x.astype(BF16)
    r1 = x - hi.astype(F32)
    mid = r1.astype(BF16)
    lo = (r1 - mid.astype(F32)).astype(BF16)
    return hi, mid, lo


def _silu(x):
    return x * (1.0 / (1.0 + jnp.exp(-x)))


def _softplus(x):
    return jnp.maximum(x, 0.0) + jnp.log1p(jnp.exp(-jnp.abs(x)))


def _rms_scale(x):
    return x * lax.rsqrt(jnp.mean(x * x, axis=-1, keepdims=True) + EPS)


def _group_top3_select(gate, valid):
    lane = lax.broadcasted_iota(jnp.int32, gate.shape, 1)
    n = lane % 8
    beats = jnp.zeros(gate.shape, F32)
    for r in range(1, 8):
        same = n >= r
        partner = jnp.where(same, pltpu.roll(gate, r, 1), pltpu.roll(gate, LANES - 8 + r, 1))
        beats = beats + jnp.where(same, jnp.where(partner >= gate, 1.0, 0.0), jnp.where(partner > gate, 1.0, 0.0))
    return jnp.where(valid, beats, float(MOBA_TOPK)) < float(MOBA_TOPK)


def _in_proj_kernel(kv_transposed, x_ref, g_ref, wqkv_ref, wz_ref, wxbc_ref, wdt_ref, wdtT_ref,
                    cos_ref, sina_ref, sinb_ref,
                    qb_ref, kb_ref, vb_ref, k_ref, v_ref, z_ref, xbc_ref, dt_ref, dtT_ref, kmean_ref):
    x = x_ref[...]
    xn = (_rms_scale(x) * g_ref[...]).astype(BF16)
    cw = 512
    reps = cw // LANES
    cos = jnp.tile(cos_ref[...], (1, reps))
    sina = jnp.tile(sina_ref[...], (1, reps))
    sinb = jnp.tile(sinb_ref[...], (1, reps))
    for c in range(3 * D_MODEL // cw):
        r = _nn(xn, wqkv_ref[:, c * cw:(c + 1) * cw])
        lo = (c * cw) % D_MODEL
        if c < 2 * D_MODEL // cw:
            r = r * cos + pltpu.roll(r, cw - ROT_DIM // 2, 1) * sina + pltpu.roll(r, ROT_DIM // 2, 1) * sinb
        if c < D_MODEL // cw:
            qb_ref[:, lo:lo + cw] = (r * (HEAD_DIM ** -0.5)).astype(BF16)
        elif c < 2 * D_MODEL // cw:
            if kv_transposed:
                k_ref[0, lo:lo + cw, :] = r.T
            else:
                k_ref[:, lo:lo + cw] = r
            kb_ref[:, lo:lo + cw] = r.astype(BF16)
            kmean_ref[0, :, lo:lo + cw] = jnp.sum(r, axis=0, keepdims=True) * (1.0 / MOBA_BLOCK)
        else:
            if kv_transposed:
                v_ref[0, lo:lo + cw, :] = r.T
            else:
                v_ref[:, lo:lo + cw] = r
            vb_ref[:, lo:lo + cw] = r.astype(BF16)
    for c in range(D_MODEL // cw):
        z_ref[:, c * cw:(c + 1) * cw] = _nn(xn, wz_ref[:, c * cw:(c + 1) * cw])
    for c in range(CONV_DIM // cw):
        xbc_ref[:, c * cw:(c + 1) * cw] = _nn(xn, wxbc_ref[:, c * cw:(c + 1) * cw])
    dt_ref[...] = _nn(xn, wdt_ref[...])
    dtT_ref[...] = _nt(wdtT_ref[...], xn)


def _in_proj(x2d, g, w, tabs, tm, tab_period, n_seq=0):
    t = x2d.shape[0]
    nt = t // tm
    if n_seq:
        per_seq = nt // n_seq
        kv_shape = jax.ShapeDtypeStruct((n_seq, D_MODEL, t // n_seq), F32)
        kv_spec = pl.BlockSpec((1, D_MODEL, tm), lambda i: (i // per_seq, 0, i % per_seq))
    else:
        kv_shape = jax.ShapeDtypeStruct((t, D_MODEL), F32)
        kv_spec = pl.BlockSpec((tm, D_MODEL), lambda i: (i, 0))
    full = lambda shape: pl.BlockSpec(shape, lambda i: (0,) * len(shape), pipeline_mode=pl.Buffered(1))
    rows = lambda width: pl.BlockSpec((tm, width), lambda i: (i, 0))
    tab = pl.BlockSpec((tm, LANES), lambda i: (i % tab_period, 0))
    out_shape = (
        jax.ShapeDtypeStruct((t, D_MODEL), BF16), jax.ShapeDtypeStruct((t, D_MODEL), BF16),
        jax.ShapeDtypeStruct((t, D_MODEL), BF16),
        kv_shape, kv_shape,
        jax.ShapeDtypeStruct((t, D_MODEL), F32), jax.ShapeDtypeStruct((t, CONV_DIM), F32),
        jax.ShapeDtypeStruct((t, SSM_HEADS), F32), jax.ShapeDtypeStruct((SSM_HEADS, t), F32),
        jax.ShapeDtypeStruct((nt, 1, D_MODEL), F32),
    )
    out_specs = (rows(D_MODEL), rows(D_MODEL), rows(D_MODEL), kv_spec, kv_spec,
                 rows(D_MODEL), rows(CONV_DIM), rows(SSM_HEADS),
                 pl.BlockSpec((SSM_HEADS, tm), lambda i: (0, i)),
                 pl.BlockSpec((1, 1, D_MODEL), lambda i: (i, 0, 0)))
    return pl.pallas_call(
        functools.partial(_in_proj_kernel, bool(n_seq)),
        out_shape=out_shape,
        grid=(nt,),
        in_specs=[rows(D_MODEL), full((1, D_MODEL)), full(w['wqkv'].shape), full(w['wz'].shape),
                  full(w['wxbc'].shape), full(w['wdt'].shape), full(w['wdtT'].shape), tab, tab, tab],
        out_specs=out_specs,
        compiler_params=pltpu.CompilerParams(dimension_semantics=("arbitrary",), vmem_limit_bytes=VMEM_LIMIT),
        name="in_proj",
    )(x2d, g, w['wqkv'], w['wz'], w['wxbc'], w['wdt'], w['wdtT'], *tabs)


def _rope_tables(pos):
    half = ROT_DIM // 2
    inv_freq = ROPE_THETA ** (-jnp.arange(half, dtype=F32) / half)
    ang = pos.astype(F32)[:, None] * inv_freq[None, :]
    cos, sin = jnp.cos(ang), jnp.sin(ang)
    p = pos.shape[0]
    pad = jnp.zeros((p, HEAD_DIM - ROT_DIM), F32)
    zero = jnp.zeros((p, half), F32)
    cos_t = jnp.concatenate([cos, cos, pad + 1.0], axis=1)
    sina_t = jnp.concatenate([-sin, zero, pad], axis=1)
    sinb_t = jnp.concatenate([zero, sin, pad], axis=1)
    rep = LANES // HEAD_DIM
    return tuple(jnp.tile(a, (1, rep)) for a in (cos_t, sina_t, sinb_t))


def _moba_prompt_kernel(q_ref, k_ref, v_ref, means_ref, g_ref, o_ref, lhs_ref, m_ref, l_ref, acc_ref, out_ref):
    qi = pl.program_id(1)
    tq = ATTN_TQ
    per_blk = MOBA_BLOCK // tq
    qblk = qi // per_blk
    row0 = (qi % per_blk) * tq
    q = q_ref[...]

    means = means_ref[0]
    mrow = lax.broadcasted_iota(jnp.int32, (LANES, D_MODEL), 0)
    mcol = lax.broadcasted_iota(jnp.int32, (LANES, D_MODEL), 1)
    mb = jnp.where(mrow // 8 == mcol // HEAD_DIM, jnp.tile(means, (LANES // 8, 1)), 0.0).astype(BF16)
    gate = _nt(q, mb)
    lane = lax.broadcasted_iota(jnp.int32, (tq, LANES), 1)
    valid = (lane % 8) < qblk
    gate = jnp.where(valid, gate, NEG_INF)
    sel = _group_top3_select(gate, valid)
    selbias = jnp.where(sel, 0.0, NEG_INF)

    krow = lax.broadcasted_iota(jnp.int32, (2 * tq, MOBA_BLOCK), 1)
    qrow = lax.broadcasted_iota(jnp.int32, (2 * tq, MOBA_BLOCK), 0) % tq + row0
    causal = krow <= qrow
    klane = lax.broadcasted_iota(jnp.int32, (MOBA_BLOCK, LANES), 1)
    kstart = pl.multiple_of(qblk * MOBA_BLOCK, MOBA_BLOCK)
    ones = jnp.ones((MOBA_BLOCK, LANES), BF16)
    n_pairs = N_HEADS // 2

    for hp in range(n_pairs):
        cols = slice(hp * LANES, (hp + 1) * LANES)
        qp = q[:, cols].astype(F32)
        rows = []
        for sub in range(2):
            h = 2 * hp + sub
            head_lanes = (lane < HEAD_DIM) if sub == 0 else (lane >= HEAD_DIM)
            bias = selbias if h == 0 else pltpu.roll(selbias, LANES - 8 * h, 1)
            rows.append(jnp.concatenate([jnp.where(head_lanes, qp, 0.0), bias], axis=1))
        lhs = jnp.concatenate(rows, axis=0).astype(BF16)
        lhs_ref[hp] = lhs
        kd = k_ref[pl.ds(kstart, MOBA_BLOCK), cols]
        vd = v_ref[pl.ds(kstart, MOBA_BLOCK), cols]
        s = jnp.where(causal, _nt(lhs[:, :LANES], kd), NEG_INF)
        m0 = jnp.max(s, axis=1, keepdims=True)
        p = jnp.exp(s - m0)
        pv = _nn(p.astype(BF16), jnp.concatenate([vd, ones], axis=1))
        m_ref[hp] = jnp.broadcast_to(m0, (2 * tq, LANES))
        acc_ref[hp] = pv[:, :LANES]
        l_ref[hp] = pv[:, LANES:]

    def body(j, carry):
        ks = pl.multiple_of(j * MOBA_BLOCK, MOBA_BLOCK)
        onehot = jnp.where(klane == j, 1.0, 0.0).astype(BF16)
        for hp in range(n_pairs):
            cols = slice(hp * LANES, (hp + 1) * LANES)
            kb = k_ref[pl.ds(ks, MOBA_BLOCK), cols]
            vb = v_ref[pl.ds(ks, MOBA_BLOCK), cols]
            s = _nt(lhs_ref[hp], jnp.concatenate([kb, onehot], axis=1))
            m_old = m_ref[hp]
            m_new = jnp.maximum(m_old, jnp.max(s, axis=1, keepdims=True))
            alpha = jnp.exp(m_old - m_new)
            p = jnp.exp(s - jnp.concatenate([m_new, m_new], axis=1))
            pv = _nn(p.astype(BF16), jnp.concatenate([vb, ones], axis=1))
            m_ref[hp] = m_new
            acc_ref[hp] = alpha * acc_ref[hp] + pv[:, :LANES]
            l_ref[hp] = alpha * l_ref[hp] + pv[:, LANES:]
        return carry

    lax.fori_loop(0, qblk, body, 0)

    for hp in range(n_pairs):
        o = acc_ref[hp] / l_ref[hp]
        out_ref[:, hp * LANES:(hp + 1) * LANES] = jnp.where(lane < HEAD_DIM, o[:tq], o[tq:])
    a = out_ref[...]
    o_ref[...] = (_rms_scale(a) * g_ref[...]).astype(BF16)


def _moba_prompt(qb, kb, vb, means, g_attn, bn, s):
    nq = s // ATTN_TQ
    nb = means.shape[1]
    return pl.pallas_call(
        _moba_prompt_kernel,
        out_shape=jax.ShapeDtypeStruct((bn * s, D_MODEL), BF16),
        grid=(bn, nq),
        in_specs=[pl.BlockSpec((ATTN_TQ, D_MODEL), lambda b, i: (b * nq + i, 0)),
                  pl.BlockSpec((s, D_MODEL), lambda b, i: (b, 0)),
                  pl.BlockSpec((s, D_MODEL), lambda b, i: (b, 0)),
                  pl.BlockSpec((1, nb, D_MODEL), lambda b, i: (b, 0, 0)),
                  pl.BlockSpec((1, D_MODEL), lambda b, i: (0, 0))],
        out_specs=pl.BlockSpec((ATTN_TQ, D_MODEL), lambda b, i: (b * nq + i, 0)),
        scratch_shapes=[pltpu.VMEM((N_HEADS // 2, 2 * ATTN_TQ, 2 * LANES), BF16),
                        pltpu.VMEM((N_HEADS // 2, 2 * ATTN_TQ, LANES), F32),
                        pltpu.VMEM((N_HEADS // 2, 2 * ATTN_TQ, LANES), F32),
                        pltpu.VMEM((N_HEADS // 2, 2 * ATTN_TQ, LANES), F32),
                        pltpu.VMEM((ATTN_TQ, D_MODEL), F32)],
        compiler_params=pltpu.CompilerParams(dimension_semantics=("arbitrary", "arbitrary"),
                                             vmem_limit_bytes=VMEM_LIMIT),
        name="moba_prompt",
    )(qb, kb, vb, means, g_attn)


def _moba_sample_kernel(pt_ref, q_ref, kn_ref, vn_ref, *refs):
    del pt_ref
    pps = SAMPLE_BLOCKS_PER_STEP * (MOBA_BLOCK // PAGE_SIZE)
    k_pages, v_pages = refs[:pps], refs[pps:2 * pps]
    gcol_ref, o_ref, tok_ref, stat_ref, part_ref, outT_ref = refs[2 * pps:]
    b = pl.program_id(0)
    step = pl.program_id(1)
    db = pl.num_programs(0)
    nsteps = pl.num_programs(1)
    nb = nsteps * SAMPLE_BLOCKS_PER_STEP
    hd3 = (N_HEADS, HEAD_DIM, LANES)

    @pl.when(step == 0)
    def _():
        r = lax.broadcasted_iota(jnp.int32, (q_ref.shape[0], LANES), 0)
        onehot = jnp.where(r == b, 1.0, 0.0).astype(BF16)
        tok_ref[0] = _tn(q_ref[...], onehot)
        tok_ref[1] = _tn(kn_ref[...], onehot)
        tok_ref[2] = _tn(vn_ref[...], onehot)

    @pl.when(jnp.logical_and(step == 0, b == 0))
    def _():
        outT_ref[...] = jnp.zeros(outT_ref.shape, F32)

    q3 = tok_ref[0].reshape(hd3)
    for kk in range(SAMPLE_BLOCKS_PER_STEP):
        n = step * SAMPLE_BLOCKS_PER_STEP + kk
        ka_ref, kb_ref, va_ref, vb_ref = k_pages[2 * kk], k_pages[2 * kk + 1], v_pages[2 * kk], v_pages[2 * kk + 1]
        s_a = jnp.sum(ka_ref[0] * q3, axis=1)
        s_b = jnp.sum(kb_ref[0] * q3, axis=1)
        m = jnp.maximum(jnp.max(s_a, axis=1, keepdims=True), jnp.max(s_b, axis=1, keepdims=True))
        p_a = jnp.exp(s_a - m)
        p_b = jnp.exp(s_b - m)
        l = jnp.sum(p_a, axis=1, keepdims=True) + jnp.sum(p_b, axis=1, keepdims=True)
        gate = (jnp.sum(s_a, axis=1, keepdims=True) + jnp.sum(s_b, axis=1, keepdims=True)) * (1.0 / MOBA_BLOCK)
        part_ref[n] = p_a[:, None, :] * va_ref[0] + p_b[:, None, :] * vb_ref[0]
        stat_ref[n, 0] = jnp.broadcast_to(m, (N_HEADS, LANES))
        stat_ref[n, 1] = jnp.broadcast_to(l, (N_HEADS, LANES))
        stat_ref[n, 2] = jnp.broadcast_to(gate, (N_HEADS, LANES))

    @pl.when(step == nsteps - 1)
    def _():
        lane = lax.broadcasted_iota(jnp.int32, (N_HEADS, LANES), 1)
        gate_all = jnp.zeros((N_HEADS, LANES), F32)
        for j in range(8):
            gate_all = jnp.where(lane == j, stat_ref[j, 2], gate_all)
        valid = lane < nb
        sel = _group_top3_select(jnp.where(valid, gate_all, NEG_INF), valid)
        self3 = jnp.where(lax.broadcasted_iota(jnp.int32, hd3, 2) == 0, tok_ref[2].reshape(hd3), 0.0)
        s_self = jnp.sum(q3 * tok_ref[1].reshape(hd3), axis=1)
        sel_f = jnp.where(sel, 1.0, 0.0)
        sel_j = [jnp.broadcast_to(sel_f[:, j:j + 1], (N_HEADS, LANES)) > 0.5 for j in range(8)]
        mtot = s_self
        for j in range(8):
            mtot = jnp.maximum(mtot, jnp.where(sel_j[j], stat_ref[j, 0], NEG_INF))
        w_self = jnp.exp(s_self - mtot)
        den = w_self
        num = w_self[:, None, :] * self3
        for j in range(8):
            wj = jnp.where(sel_j[j], jnp.exp(stat_ref[j, 0] - mtot), 0.0)
            den = den + wj * stat_ref[j, 1]
            num = num + wj[:, None, :] * part_ref[j]
        out = jnp.broadcast_to(jnp.sum(num, axis=2, keepdims=True), hd3) / den[:, None, :]
        ms = jnp.sum(jnp.sum(out * out, axis=1, keepdims=True), axis=0, keepdims=True) * (1.0 / D_MODEL)
        out = (out * lax.rsqrt(ms + EPS)).reshape(D_MODEL, LANES) * gcol_ref[...]
        col = lax.broadcasted_iota(jnp.int32, (D_MODEL, LANES), 1)
        outT_ref[...] = jnp.where(col == b, out, outT_ref[...])

    @pl.when(jnp.logical_and(step == nsteps - 1, b == db - 1))
    def _():
        o_ref[...] = outT_ref[...].T[:o_ref.shape[0]].astype(BF16)


def _moba_sample(qb, kb_new, vb_new, cache_k, cache_v, page_table, g_attn):
    db = qb.shape[0]
    n_pages = page_table.shape[1]
    ppb = MOBA_BLOCK // PAGE_SIZE
    nb = n_pages // ppb
    pt = page_table.reshape(-1)
    pool_k = jnp.transpose(cache_k, (0, 2, 3, 1))
    pool_v = jnp.transpose(cache_v, (0, 2, 3, 1))
    g_col = jnp.broadcast_to(g_attn.reshape(D_MODEL, 1), (D_MODEL, LANES))
    pps = SAMPLE_BLOCKS_PER_STEP * ppb
    page = lambda k: pl.BlockSpec((1, N_HEADS, HEAD_DIM, PAGE_SIZE),
                                  lambda b, n, pt: (pt[b * n_pages + pps * n + k], 0, 0, 0))
    whole = lambda shape: pl.BlockSpec(shape, lambda b, n, pt: (0,) * len(shape))
    pages = [page(k) for k in range(pps)]
    grid_spec = pltpu.PrefetchScalarGridSpec(
        num_scalar_prefetch=1,
        grid=(db, nb // SAMPLE_BLOCKS_PER_STEP),
        in_specs=[whole((db, D_MODEL)), whole((db, D_MODEL)), whole((db, D_MODEL))] + pages + pages
                 + [whole((D_MODEL, LANES))],
        out_specs=whole((db, D_MODEL)),
        scratch_shapes=[pltpu.VMEM((3, D_MODEL, LANES), F32),
                        pltpu.VMEM((8, 3, N_HEADS, LANES), F32),
                        pltpu.VMEM((8, N_HEADS, HEAD_DIM, LANES), F32),
                        pltpu.VMEM((D_MODEL, LANES), F32)],
    )
    return pl.pallas_call(
        _moba_sample_kernel,
        out_shape=jax.ShapeDtypeStruct((db, D_MODEL), BF16),
        grid_spec=grid_spec,
        compiler_params=pltpu.CompilerParams(dimension_semantics=("arbitrary", "arbitrary"),
                                             vmem_limit_bytes=VMEM_LIMIT),
        name="moba_sample",
    )(pt, qb, kb_new, vb_new, *([pool_k] * pps), *([pool_v] * pps), g_col)


def _ssd_prompt_kernel(xbc_ref, z_ref, dt_ref, dtT_ref, cw_ref, cb_ref, dtb_ref, dtbT_ref, alog_ref,
                       alogT_ref, dskip_ref, gssd_ref, y_ref, conv_ref, ssm_ref, h_ref, xp_ref):
    c = pl.program_id(1)
    q = SSD_CHUNK
    halo = SUBLANES

    @pl.when(c == 0)
    def _():
        h_ref[...] = jnp.zeros(h_ref.shape, F32)
        xp_ref[0:halo, :] = jnp.zeros((halo, CONV_DIM), F32)

    xp_ref[halo:halo + q, :] = xbc_ref[...]
    acc = jnp.broadcast_to(cb_ref[...], (q, CONV_DIM))
    for w in range(CONV_W):
        acc = acc + xp_ref[pl.ds(halo - (CONV_W - 1) + w, q), :] * cw_ref[w:w + 1, :]
    xc = _silu(acc)

    @pl.when(c == pl.num_programs(1) - 1)
    def _():
        conv_ref[0] = xp_ref[halo + q - (CONV_W - 1):halo + q, :]

    xp_ref[0:halo, :] = xp_ref[q:q + halo, :]

    xs = xc[:, :SSM_WIDTH]
    bm = xc[:, SSM_WIDTH:SSM_WIDTH + SSM_GROUPS * D_STATE].astype(BF16)
    cm = xc[:, SSM_WIDTH + SSM_GROUPS * D_STATE:].astype(BF16)

    dt = _softplus(dt_ref[...] + dtb_ref[...])
    dtT = _softplus(dtT_ref[...] + dtbT_ref[...])
    d_a = dt * (-jnp.exp(alog_ref[...]))
    d_aT = dtT * (-jnp.exp(alogT_ref[...]))
    trow = lax.broadcasted_iota(jnp.int32, (q, q), 0)
    tcol = lax.broadcasted_iota(jnp.int32, (q, q), 1)
    tri = tcol <= trow
    tri_b = jnp.where(tri, 1.0, 0.0).astype(BF16)
    acs = sum(_nn(tri_b, part) for part in _split3(d_a))
    acsT = sum(_nt(part, tri_b) for part in _split3(d_aT))
    acs_last = acs[q - 1:q, :]

    lane = lax.broadcasted_iota(jnp.int32, (q, LANES), 1)
    rowi = lax.broadcasted_iota(jnp.int32, (2 * SSM_HEAD_DIM, 1), 0)
    left = lane < SSM_HEAD_DIM
    hg = SSM_HEADS // SSM_GROUPS
    cb_mat = [None] * SSM_GROUPS
    for hp in range(SSM_HEADS // 2):
        g = (2 * hp) // hg
        bg = bm[:, g * D_STATE:(g + 1) * D_STATE]
        cg = cm[:, g * D_STATE:(g + 1) * D_STATE]
        if cb_mat[g] is None:
            cb_mat[g] = _nt(cg, bg)
        cols = slice(hp * LANES, (hp + 1) * LANES)
        x_pair = xs[:, cols]
        x_pair_b = x_pair.astype(BF16)
        h_pair = h_ref[2 * hp:2 * hp + 2].reshape(2 * SSM_HEAD_DIM, D_STATE)
        y_off = _nt(cg, h_pair.astype(BF16))
        y_in, e_col, te_col, dec = [], [], [], []
        for sub in range(2):
            h = 2 * hp + sub
            col = acs[:, h:h + 1]
            seg = col - acsT[h:h + 1, :]
            decay = jnp.where(tri, jnp.exp(jnp.minimum(seg, 0.0)), 0.0)
            wm = (cb_mat[g] * decay * dtT[h:h + 1, :]).astype(BF16)
            y_in.append(_nn(wm, x_pair_b))
            e_col.append(jnp.exp(col))
            te_col.append(jnp.exp(acs_last[:, h:h + 1] - col) * dt[:, h:h + 1])
            dec.append(jnp.exp(acs_last[:, h:h + 1]))
        y_pair = jnp.where(left, y_in[0], y_in[1]) + y_off * jnp.where(left, e_col[0], e_col[1])
        xw = (x_pair * jnp.where(left, te_col[0], te_col[1])).astype(BF16)
        upd = _tn(xw, bg)
        h_new = h_pair * jnp.where(rowi < SSM_HEAD_DIM, dec[0], dec[1]) + upd
        h_ref[2 * hp:2 * hp + 2] = h_new.reshape(2, SSM_HEAD_DIM, D_STATE)
        yp = y_pair + dskip_ref[:, cols] * x_pair
        zp = z_ref[:, cols]
        xp_ref[halo:halo + q, cols] = yp * _silu(zp)

    gw = SSM_WIDTH // SSM_GROUPS
    for g in range(SSM_GROUPS):
        yg = xp_ref[halo:halo + q, g * gw:(g + 1) * gw]
        y_ref[:, g * gw:(g + 1) * gw] = (_rms_scale(yg) * gssd_ref[:, g * gw:(g + 1) * gw]).astype(BF16)

    @pl.when(c == pl.num_programs(1) - 1)
    def _():
        ssm_ref[0] = h_ref[...]


def _ssd_prompt(xbc, z, dt, dtT, p, bn, s):
    nc = s // SSD_CHUNK
    q = SSD_CHUNK
    const = lambda shape: pl.BlockSpec(shape, lambda b, c: (0,) * len(shape))
    rows = lambda width: pl.BlockSpec((q, width), lambda b, c: (b * nc + c, 0))
    return pl.pallas_call(
        _ssd_prompt_kernel,
        out_shape=(jax.ShapeDtypeStruct((bn * s, SSM_WIDTH), BF16),
                   jax.ShapeDtypeStruct((bn, CONV_W - 1, CONV_DIM), F32),
                   jax.ShapeDtypeStruct((bn, SSM_HEADS, SSM_HEAD_DIM, D_STATE), F32)),
        grid=(bn, nc),
        in_specs=[rows(CONV_DIM), rows(SSM_WIDTH), rows(SSM_HEADS),
                  pl.BlockSpec((SSM_HEADS, q), lambda b, c: (0, b * nc + c)),
                  const((CONV_W, CONV_DIM)), const((1, CONV_DIM)), const((1, SSM_HEADS)),
                  const((SSM_HEADS, 1)), const((1, SSM_HEADS)), const((SSM_HEADS, 1)),
                  const((1, SSM_WIDTH)), const((1, SSM_WIDTH))],
        out_specs=(rows(SSM_WIDTH),
                   pl.BlockSpec((1, CONV_W - 1, CONV_DIM), lambda b, c: (b, 0, 0)),
                   pl.BlockSpec((1, SSM_HEADS, SSM_HEAD_DIM, D_STATE), lambda b, c: (b, 0, 0, 0))),
        scratch_shapes=[pltpu.VMEM((SSM_HEADS, SSM_HEAD_DIM, D_STATE), F32),
                        pltpu.VMEM((SUBLANES + q, CONV_DIM), F32)],
        compiler_params=pltpu.CompilerParams(dimension_semantics=("arbitrary", "arbitrary"),
                                             vmem_limit_bytes=VMEM_LIMIT),
        name="ssd_prompt",
    )(xbc, z, dt, dtT, p['conv_w'], p['conv_b'], p['dt_bias'], p['dt_biasT'], p['a_log'], p['a_logT'],
      p['d_skip'], p['g_ssd'])


def _ssd_sample_pre_kernel(sc_ref, xbc_ref, dt_ref, cw_ref, cb_ref, dtb_ref, alog_ref, ex_ref,
                           conv_ref, xs_ref, b_ref, c_ref, dtxT_ref, decT_ref):
    acc = jnp.broadcast_to(cb_ref[...], xbc_ref.shape)
    for w in range(CONV_W - 1):
        acc = acc + sc_ref[:, w * CONV_DIM:(w + 1) * CONV_DIM] * cw_ref[w:w + 1, :]
    xnew = xbc_ref[...]
    acc = acc + xnew * cw_ref[CONV_W - 1:CONV_W, :]
    xc = _silu(acc)
    conv_ref[:, 0:(CONV_W - 2) * CONV_DIM] = sc_ref[:, CONV_DIM:(CONV_W - 1) * CONV_DIM]
    conv_ref[:, (CONV_W - 2) * CONV_DIM:] = xnew
    xs = xc[:, :SSM_WIDTH]
    xs_ref[...] = xs
    b_ref[...] = xc[:, SSM_WIDTH:SSM_WIDTH + SSM_GROUPS * D_STATE]
    c_ref[...] = xc[:, SSM_WIDTH + SSM_GROUPS * D_STATE:]
    dt = _softplus(dt_ref[...] + dtb_ref[...])
    d_a = dt * (-jnp.exp(alog_ref[...]))
    ex = ex_ref[...]
    dt_l = sum(_nn(part, ex) for part in _split3(dt))
    da_l = sum(_nn(part, ex) for part in _split3(d_a))
    dtxT_ref[...] = (dt_l * xs).T
    decT_ref[...] = jnp.exp(da_l).T


def _ssd_sample_step_kernel(dtxT_ref, decT_ref, b_ref, c_ref, h_ref, hn_ref, y_ref):
    b = pl.program_id(0)
    db = dtxT_ref.shape[1]
    r = lax.broadcasted_iota(jnp.int32, (db, LANES), 0)
    onehot = jnp.where(r == b, 1.0, 0.0).astype(BF16)
    dtx_bc = sum(_nn(part, onehot) for part in _split3(dtxT_ref[...]))
    dec_bc = sum(_nn(part, onehot) for part in _split3(decT_ref[...]))
    gw = SSM_WIDTH // SSM_GROUPS
    base = pl.multiple_of((b // SUBLANES) * SUBLANES, SUBLANES)
    sub = lax.broadcasted_iota(jnp.int32, (SUBLANES, SSM_GROUPS * D_STATE), 0)
    pick = sub == b % SUBLANES
    b_all = jnp.sum(jnp.where(pick, b_ref[pl.ds(base, SUBLANES), :], 0.0), axis=0, keepdims=True)
    c_all = jnp.sum(jnp.where(pick, c_ref[pl.ds(base, SUBLANES), :], 0.0), axis=0, keepdims=True)
    rows = []
    for g in range(SSM_GROUPS):
        b_row = b_all[:, g * D_STATE:(g + 1) * D_STATE]
        c_row = c_all[:, g * D_STATE:(g + 1) * D_STATE].astype(BF16)
        hg = h_ref[0, g * (SSM_HEADS // SSM_GROUPS):(g + 1) * (SSM_HEADS // SSM_GROUPS)].reshape(gw, D_STATE)
        h_new = hg * dec_bc[g * gw:(g + 1) * gw] + dtx_bc[g * gw:(g + 1) * gw] * b_row
        hn_ref[0, g * (SSM_HEADS // SSM_GROUPS):(g + 1) * (SSM_HEADS // SSM_GROUPS)] = h_new.reshape(
            SSM_HEADS // SSM_GROUPS, SSM_HEAD_DIM, D_STATE)
        rows.append(_nt(jnp.broadcast_to(c_row, (SUBLANES, D_STATE)), h_new.astype(BF16))[0:1])
    y_ref[0] = jnp.concatenate(rows, axis=1)


def _ssd_sample_post_kernel(y_ref, xs_ref, z_ref, dskip_ref, gssd_ref, o_ref):
    y = (y_ref[...] + dskip_ref[...] * xs_ref[...]) * _silu(z_ref[...])
    gw = SSM_WIDTH // SSM_GROUPS
    for g in range(SSM_GROUPS):
        yg = y[:, g * gw:(g + 1) * gw]
        o_ref[:, g * gw:(g + 1) * gw] = (_rms_scale(yg) * gssd_ref[:, g * gw:(g + 1) * gw]).astype(BF16)


def _whole(shape):
    return pl.BlockSpec(shape, lambda *_: (0,) * len(shape))


def _ssd_sample(state_conv, xbc, z, dt, state_ssm, p):
    db = xbc.shape[0]
    sc2 = state_conv.reshape(db, (CONV_W - 1) * CONV_DIM)
    ex = (jnp.arange(SSM_WIDTH)[None, :] // SSM_HEAD_DIM == jnp.arange(SSM_HEADS)[:, None]).astype(BF16)
    pre_in = (sc2, xbc, dt, p['conv_w'], p['conv_b'], p['dt_bias'], p['a_log'], ex)
    pre_out = (jax.ShapeDtypeStruct(sc2.shape, F32), jax.ShapeDtypeStruct((db, SSM_WIDTH), F32),
               jax.ShapeDtypeStruct((db, SSM_GROUPS * D_STATE), F32),
               jax.ShapeDtypeStruct((db, SSM_GROUPS * D_STATE), F32),
               jax.ShapeDtypeStruct((SSM_WIDTH, db), F32), jax.ShapeDtypeStruct((SSM_WIDTH, db), F32))
    conv_new, xs, bm, cm, dtxT, decT = pl.pallas_call(
        _ssd_sample_pre_kernel, out_shape=pre_out, grid=(1,),
        in_specs=[_whole(a.shape) for a in pre_in], out_specs=tuple(_whole(o.shape) for o in pre_out),
        compiler_params=pltpu.CompilerParams(vmem_limit_bytes=VMEM_LIMIT), name="ssd_sample_pre",
    )(*pre_in)
    st = pl.BlockSpec((1, SSM_HEADS, SSM_HEAD_DIM, D_STATE), lambda b: (b, 0, 0, 0))
    ssm_new, y = pl.pallas_call(
        _ssd_sample_step_kernel,
        out_shape=(jax.ShapeDtypeStruct(state_ssm.shape, F32), jax.ShapeDtypeStruct((db, 1, SSM_WIDTH), F32)),
        grid=(db,),
        in_specs=[_whole(dtxT.shape), _whole(decT.shape), _whole(bm.shape), _whole(cm.shape), st],
        out_specs=(st, pl.BlockSpec((1, 1, SSM_WIDTH), lambda b: (b, 0, 0))),
        compiler_params=pltpu.CompilerParams(dimension_semantics=("arbitrary",), vmem_limit_bytes=VMEM_LIMIT),
        name="ssd_sample_step",
    )(dtxT, decT, bm, cm, state_ssm)
    post_in = (y.reshape(db, SSM_WIDTH), xs, z, p['d_skip'], p['g_ssd'])
    y_n = pl.pallas_call(
        _ssd_sample_post_kernel, out_shape=jax.ShapeDtypeStruct((db, SSM_WIDTH), BF16), grid=(1,),
        in_specs=[_whole(a.shape) for a in post_in], out_specs=_whole((db, SSM_WIDTH)),
        name="ssd_sample_post",
    )(*post_in)
    return y_n, conv_new.reshape(db, CONV_W - 1, CONV_DIM), ssm_new


def _merge_exchange_pairs(n):
    t = n.bit_length() - 1
    pairs = []
    p = 1 << (t - 1)
    while p > 0:
        q, r, d = 1 << (t - 1), 0, p
        while d > 0:
            for i in range(n - d):
                if (i & p) == r:
                    pairs.append((i, i + d))
            d, q, r = q - p, q >> 1, p
        p >>= 1
    return pairs


_SORT16 = _merge_exchange_pairs(PEER_TOPK)


def _sort_desc(v):
    v = list(v)
    for i, j in _SORT16:
        hi, lo = jnp.maximum(v[i], v[j]), jnp.minimum(v[i], v[j])
        v[i], v[j] = hi, lo
    return v


def _bitonic_merge_desc(v):
    v = list(v)
    stride = len(v) // 2
    while stride >= 1:
        for i in range(len(v)):
            if (i & stride) == 0:
                hi, lo = jnp.maximum(v[i], v[i + stride]), jnp.minimum(v[i], v[i + stride])
                v[i], v[i + stride] = hi, lo
        stride //= 2
    return v


def _top_merge(a, b):
    k = len(a)
    return [jnp.maximum(a[i], b[k - 1 - i]) for i in range(k)]


def _top16_desc(s):
    v = _sort_desc([s[SUBLANES * r:SUBLANES * (r + 1), :] for r in range(PEER_NKEYS // SUBLANES)])
    for shift in (4, 2, 1):
        v = _bitonic_merge_desc(_top_merge(v, [pltpu.roll(x, shift, 0) for x in v]))
    return v


def _peer_front_kernel(x_ref, attn_ref, yssd_ref, wout_ref, gffn_ref, wq_ref, keys_ref,
                       h1_ref, xn_ref, r1_ref, e1_ref, k_ref, e0_ref, s_ref):
    tm = x_ref.shape[0]
    h1 = (x_ref[...] + _nn(attn_ref[...], wout_ref[0:D_MODEL, :])
          + _nn(yssd_ref[...], wout_ref[D_MODEL:2 * D_MODEL, :]))
    h1_ref[...] = h1
    xn = (_rms_scale(h1) * gffn_ref[...]).astype(BF16)
    xn_ref[...] = xn
    sub = lax.broadcasted_iota(jnp.int32, (SUBLANES, tm), 0)
    packed = [[None] * PEER_TOPK, [None] * PEER_TOPK]
    for hh in range(2 * PEER_HEADS):
        h, half = hh // 2, hh % 2
        qh = _nn(xn, wq_ref[:, hh * PEER_NKEYS:(hh + 1) * PEER_NKEYS]).astype(BF16)
        st = _nt(keys_ref[hh], qh)
        s_ref[hh] = st
        top = _top16_desc(st)
        for a in range(PEER_TOPK):
            packed[half][a] = top[a] if h == 0 else jnp.where(sub == h, top[a], packed[half][a])
    p0, p1 = packed
    cand = lambda a, b: p0[a] + p1[b]
    g1 = [cand(0, b) for b in range(16)]
    g2 = _bitonic_merge_desc([cand(1, b) for b in range(8)] + [cand(a, 0) for a in range(15, 7, -1)])
    rest = [cand(a, b) for a in range(2, 8) for b in range(PEER_TOPK // (a + 1))]
    g3 = _sort_desc(rest[:16])
    x_hi, x_lo = jnp.maximum(rest[16], rest[17]), jnp.minimum(rest[16], rest[17])
    g3[14] = jnp.maximum(g3[14], x_lo)
    g3[15] = jnp.maximum(g3[15], x_hi)
    m12 = _bitonic_merge_desc(_top_merge(g1, g2))
    m34 = _bitonic_merge_desc(g3)
    best = _top_merge(m12, m34)
    tau = functools.reduce(jnp.minimum, best)
    top = g1[0]
    zsum = functools.reduce(lambda u, w: u + w, [jnp.exp(v - top) for v in best])
    rz = 1.0 / zsum
    for h in range(PEER_HEADS):
        s0 = s_ref[2 * h]
        s1 = s_ref[2 * h + 1]
        tau_h = tau[h:h + 1, :]
        k = jnp.zeros(s0.shape, F32)
        for b in range(PEER_TOPK):
            k = jnp.where((p1[b][h:h + 1, :] + s0) >= tau_h, float(b + 1), k)
        rank = jnp.full(s1.shape, float(PEER_TOPK), F32)
        for b in range(PEER_TOPK - 1, -1, -1):
            rank = jnp.where(p1[b][h:h + 1, :] <= s1, float(b), rank)
        k_ref[h] = k
        r1_ref[h] = rank.astype(BF16)
        e0_ref[h] = jnp.exp(s0 - p0[0][h:h + 1, :]) * rz[h:h + 1, :]
        e1_ref[h] = jnp.exp(s1 - p1[0][h:h + 1, :]).astype(BF16)


def _peer_front(x2d, attn_n, y_n, w, tm):
    t = x2d.shape[0]
    nt = t // tm
    rows = lambda width: pl.BlockSpec((tm, width), lambda i: (i, 0))
    full = lambda shape: pl.BlockSpec(shape, lambda i: (0,) * len(shape), pipeline_mode=pl.Buffered(1))
    per_head = lambda dt: jax.ShapeDtypeStruct((PEER_HEADS, PEER_NKEYS, t), dt)
    tile = pl.BlockSpec((PEER_HEADS, PEER_NKEYS, tm), lambda i: (0, 0, i))
    return pl.pallas_call(
        _peer_front_kernel,
        out_shape=(jax.ShapeDtypeStruct((t, D_MODEL), F32), jax.ShapeDtypeStruct((t, D_MODEL), BF16),
                   per_head(BF16), per_head(BF16), per_head(F32), per_head(F32)),
        grid=(nt,),
        in_specs=[rows(D_MODEL), rows(D_MODEL), rows(D_MODEL), full(w['w_out'].shape), full((1, D_MODEL)),
                  full(w['wq'].shape), full(w['keys'].shape)],
        out_specs=(rows(D_MODEL), rows(D_MODEL), tile, tile, tile, tile),
        scratch_shapes=[pltpu.VMEM((2 * PEER_HEADS, PEER_NKEYS, tm), F32)],
        compiler_params=pltpu.CompilerParams(dimension_semantics=("arbitrary",), vmem_limit_bytes=VMEM_LIMIT),
        name="peer_front",
    )(x2d, attn_n, y_n, w['w_out'], w['g_ffn'], w['wq'], w['keys'])


def _peer_dense_kernel(xn_ref, u_ref, vt_ref, r1_ref, e1_ref, k_ref, e0_ref, h1_ref, gfin_ref, y_ref,
                       acc_ref, wt_ref):
    ec = pl.program_id(1)

    @pl.when(ec == 0)
    def _():
        acc_ref[...] = jnp.zeros(acc_ref.shape, F32)

    base = pl.multiple_of(ec * PEER_IPS, PEER_IPS)
    k_rows = [k_ref[h, pl.ds(base, PEER_IPS), :] for h in range(PEER_HEADS)]
    e0_rows = [e0_ref[h, pl.ds(base, PEER_IPS), :] for h in range(PEER_HEADS)]
    xn = xn_ref[...]
    tt = xn.shape[0]
    cw = min(2 * LANES, tt)
    zero = jnp.zeros((), BF16)
    for ii in range(PEER_IPS):
        rows = slice(ii * PEER_NKEYS, (ii + 1) * PEER_NKEYS)
        act = _nt(u_ref[rows, :], xn)
        act = (0.5 * act * (1.0 + lax.erf(act * (2.0 ** -0.5)))).astype(BF16)
        for sub in range(tt // cw):
            ls = slice(sub * cw, (sub + 1) * cw)
            g = None
            for h in range(PEER_HEADS):
                kb = jnp.broadcast_to(k_rows[h][ii:ii + 1, ls], (PEER_NKEYS, cw)).astype(BF16)
                eb = jnp.broadcast_to(e0_rows[h][ii:ii + 1, ls], (PEER_NKEYS, cw)).astype(BF16)
                w = jnp.where(r1_ref[h, :, ls] < kb, e1_ref[h, :, ls], zero) * eb
                g = w if g is None else g + w
            wt_ref[rows, ls] = g * act[:, ls]
    acc_ref[...] += _nn(vt_ref[...], wt_ref[...])

    @pl.when(ec == pl.num_programs(1) - 1)
    def _():
        hf = h1_ref[...] + acc_ref[...].T
        y_ref[...] = _rms_scale(hf) * gfin_ref[...]


def _peer_dense(xn, u_b, vt_b, r1, e1, kk, e0, h1, g_final, tt):
    t = xn.shape[0]
    n_exp = u_b.shape[0]
    tile = pl.BlockSpec((PEER_HEADS, PEER_NKEYS, tt), lambda i, e: (0, 0, i))
    return pl.pallas_call(
        _peer_dense_kernel,
        out_shape=jax.ShapeDtypeStruct((t, D_MODEL), F32),
        grid=(t // tt, n_exp // PEER_EC),
        in_specs=[pl.BlockSpec((tt, D_MODEL), lambda i, e: (i, 0)),
                  pl.BlockSpec((PEER_EC, D_MODEL), lambda i, e: (e, 0)),
                  pl.BlockSpec((D_MODEL, PEER_EC), lambda i, e: (0, e)),
                  tile, tile, tile, tile,
                  pl.BlockSpec((tt, D_MODEL), lambda i, e: (i, 0)),
                  pl.BlockSpec((1, D_MODEL), lambda i, e: (0, 0))],
        out_specs=pl.BlockSpec((tt, D_MODEL), lambda i, e: (i, 0)),
        scratch_shapes=[pltpu.VMEM((D_MODEL, tt), F32), pltpu.VMEM((PEER_EC, tt), BF16)],
        compiler_params=pltpu.CompilerParams(dimension_semantics=("arbitrary", "arbitrary"),
                                             vmem_limit_bytes=VMEM_LIMIT),
        name="peer_dense",
    )(xn, u_b, vt_b, r1, e1, kk, e0, h1, g_final)


def kernel(x_prompt, x_sample, cache_k, cache_v, page_table, state_conv, state_ssm, g_mix, w_in, conv_w, conv_b, dt_bias, a_log, d_skip, g_attn, g_ssd, w_out, g_ffn, peer_wq, peer_keys, peer_u, peer_v, g_final):
    bn, s, d = x_prompt.shape
    db, t_new, _ = x_sample.shape
    n_pages = page_table.shape[1]
    assert d == D_MODEL and w_in.shape[0] == 1 and t_new == 1
    assert s % MOBA_BLOCK == 0 and s // MOBA_BLOCK <= 8 and s % SSD_CHUNK == 0
    assert n_pages * PAGE_SIZE == 8 * MOBA_BLOCK and db % LANES == 0
    past = n_pages * PAGE_SIZE

    wi = w_in[0]
    qkv_w, z_w = 3 * D_MODEL, SSM_WIDTH
    w = {
        'wqkv': wi[:, :qkv_w].astype(BF16),
        'wz': wi[:, qkv_w:qkv_w + z_w].astype(BF16),
        'wxbc': wi[:, qkv_w + z_w:qkv_w + z_w + CONV_DIM].astype(BF16),
        'wdt': wi[:, qkv_w + z_w + CONV_DIM:].astype(BF16),
        'wdtT': wi[:, qkv_w + z_w + CONV_DIM:].T.astype(BF16),
        'w_out': w_out[0].astype(BF16),
        'g_ffn': g_ffn[0].reshape(1, D_MODEL),
        'wq': peer_wq[0].astype(BF16),
        'keys': peer_keys[0].reshape(2 * PEER_HEADS, PEER_NKEYS, PEER_NKEYS).astype(BF16),
    }
    p = {
        'conv_w': conv_w[0], 'conv_b': conv_b[0].reshape(1, CONV_DIM),
        'dt_bias': dt_bias[0].reshape(1, SSM_HEADS), 'dt_biasT': dt_bias[0].reshape(SSM_HEADS, 1),
        'a_log': a_log[0].reshape(1, SSM_HEADS), 'a_logT': a_log[0].reshape(SSM_HEADS, 1),
        'd_skip': jnp.repeat(d_skip[0], SSM_HEAD_DIM).reshape(1, SSM_WIDTH),
        'g_ssd': g_ssd[0].reshape(1, SSM_WIDTH),
    }
    g_mix2 = g_mix[0].reshape(1, D_MODEL)
    g_attn2 = g_attn[0].reshape(1, D_MODEL)
    g_fin2 = g_final.reshape(1, D_MODEL)
    u_b = peer_u[0].astype(BF16)
    vt_b = peer_v[0].T.astype(BF16)

    xp2 = x_prompt.reshape(bn * s, D_MODEL)
    tabs_p = _rope_tables(jnp.arange(s, dtype=jnp.int32))
    qb, kb, vb, k_p, v_p, z_p, xbc_p, dt_p, dtT_p, kmean = _in_proj(
        xp2, g_mix2, w, tabs_p, MOBA_BLOCK, s // MOBA_BLOCK, n_seq=bn)
    nb = s // MOBA_BLOCK
    means = jnp.pad(kmean.reshape(bn, nb, D_MODEL), ((0, 0), (0, 8 - nb), (0, 0)))
    attn_p = _moba_prompt(qb, kb, vb, means, g_attn2, bn, s)
    y_p, conv_p, ssm_p = _ssd_prompt(xbc_p, z_p, dt_p, dtT_p, p, bn, s)
    h1_p, xn_p, *gate_p = _peer_front(xp2, attn_p, y_p, w, MOBA_BLOCK)
    y_prompt = _peer_dense(xn_p, u_b, vt_b, *gate_p, h1_p, g_fin2, PEER_TT)

    xs2 = x_sample.reshape(db, D_MODEL)
    tabs_s = _rope_tables(jnp.full((LANES,), past, dtype=jnp.int32))
    qs, ks, vs, k_s, v_s, z_s, xbc_s, dt_s, _, _ = _in_proj(xs2, g_mix2, w, tabs_s, LANES, 1)
    attn_s = _moba_sample(qs, ks, vs, cache_k[0], cache_v[0], page_table, g_attn2)
    y_s, conv_s, ssm_s = _ssd_sample(state_conv[0], xbc_s, z_s, dt_s, state_ssm[0], p)
    h1_s, xn_s, *gate_s = _peer_front(xs2, attn_s, y_s, w, LANES)
    y_sample = _peer_dense(xn_s, u_b, vt_b, *gate_s, h1_s, g_fin2, LANES)

    heads = (N_HEADS, HEAD_DIM)
    return (y_prompt.reshape(bn, s, D_MODEL), y_sample.reshape(db, 1, D_MODEL),
            jnp.transpose(k_p.reshape((bn,) + heads + (s,)), (0, 3, 1, 2))[None],
            jnp.transpose(v_p.reshape((bn,) + heads + (s,)), (0, 3, 1, 2))[None],
            k_s.reshape((1, db, 1) + heads), v_s.reshape((1, db, 1) + heads),
            conv_p[None], conv_s[None], ssm_p[None], ssm_s[None])
```

```python
import functools
import math

import jax
import jax.numpy as jnp
from jax import lax
from jax.experimental import pallas as pl
from jax.experimental.pallas import tpu as pltpu

F32 = jnp.float32
BF16 = jnp.bfloat16

D_MODEL = 1024
HEAD_DIM = 64
N_HEADS = 16
ROT_DIM = 16
ROPE_THETA = 500000.0
MOBA_BLOCK = 256
MOBA_TOPK = 3
PAGE_SIZE = 128
SSM_HEAD_DIM = 64
SSM_HEADS = 16
SSM_GROUPS = 2
D_STATE = 128
CONV_W = 4
SSD_CHUNK = 128
SSM_WIDTH = 1024
CONV_DIM = SSM_WIDTH + 2 * SSM_GROUPS * D_STATE
PEER_HEADS = 8
PEER_NKEYS = 128
PEER_TOPK = 16
EPS = 1e-6
NEG_INF = -1e30

LANES = 128
SUBLANES = 8
VMEM_LIMIT = 52 * 1024 * 1024
ATTN_TQ = 128
SAMPLE_BLOCKS_PER_STEP = 4
PEER_TT = 512
PEER_IPS = 2 * SUBLANES
PEER_EC = PEER_IPS * PEER_NKEYS


def _nt(a, b):
    return lax.dot_general(a, b, (((1,), (1,)), ((), ())), preferred_element_type=F32)


def _tn(a, b):
    return lax.dot_general(a, b, (((0,), (0,)), ((), ())), preferred_element_type=F32)


def _nn(a, b):
    return jnp.dot(a, b, preferred_element_type=F32)


def _split3(x):
    hi = x.astype(BF16)
    r1 = x - hi.astype(F32)
    mid = r1.astype(BF16)
    lo = (r1 - mid.astype(F32)).astype(BF16)
    return hi, mid, lo


def _silu(x):
    return x * (1.0 / (1.0 + jnp.exp(-x)))


def _softplus(x):
    return jnp.maximum(x, 0.0) + jnp.log1p(jnp.exp(-jnp.abs(x)))


def _rms_scale(x):
    return x * lax.rsqrt(jnp.mean(x * x, axis=-1, keepdims=True) + EPS)


def _group_top3_select(gate, valid):
    lane = lax.broadcasted_iota(jnp.int32, gate.shape, 1)
    n = lane % 8
    beats = jnp.zeros(gate.shape, F32)
    for r in range(1, 8):
        same = n >= r
        partner = jnp.where(same, pltpu.roll(gate, r, 1), pltpu.roll(gate, LANES - 8 + r, 1))
        beats = beats + jnp.where(same, jnp.where(partner >= gate, 1.0, 0.0), jnp.where(partner > gate, 1.0, 0.0))
    return jnp.where(valid, beats, float(MOBA_TOPK)) < float(MOBA_TOPK)


def _in_proj_kernel(kv_transposed, x_ref, g_ref, wqkv_ref, wz_ref, wxbc_ref, wdt_ref, wdtT_ref,
                    cos_ref, sina_ref, sinb_ref,
                    qb_ref, kb_ref, vb_ref, k_ref, v_ref, z_ref, xbc_ref, dt_ref, dtT_ref, kmean_ref):
    x = x_ref[...]
    xn = (_rms_scale(x) * g_ref[...]).astype(BF16)
    cw = 512
    reps = cw // LANES
    cos = jnp.tile(cos_ref[...], (1, reps))
    sina = jnp.tile(sina_ref[...], (1, reps))
    sinb = jnp.tile(sinb_ref[...], (1, reps))
    for c in range(3 * D_MODEL // cw):
        r = _nn(xn, wqkv_ref[:, c * cw:(c + 1) * cw])
        lo = (c * cw) % D_MODEL
        if c < 2 * D_MODEL // cw:
            r = r * cos + pltpu.roll(r, cw - ROT_DIM // 2, 1) * sina + pltpu.roll(r, ROT_DIM // 2, 1) * sinb
        if c < D_MODEL // cw:
            qb_ref[:, lo:lo + cw] = (r * (HEAD_DIM ** -0.5)).astype(BF16)
        elif c < 2 * D_MODEL // cw:
            if kv_transposed:
                k_ref[0, lo:lo + cw, :] = r.T
            else:
                k_ref[:, lo:lo + cw] = r
            kb_ref[:, lo:lo + cw] = r.astype(BF16)
            kmean_ref[0, :, lo:lo + cw] = jnp.sum(r, axis=0, keepdims=True) * (1.0 / MOBA_BLOCK)
        else:
            if kv_transposed:
                v_ref[0, lo:lo + cw, :] = r.T
            else:
                v_ref[:, lo:lo + cw] = r
            vb_ref[:, lo:lo + cw] = r.astype(BF16)
    for c in range(D_MODEL // cw):
        z_ref[:, c * cw:(c + 1) * cw] = _nn(xn, wz_ref[:, c * cw:(c + 1) * cw])
    for c in range(CONV_DIM // cw):
        xbc_ref[:, c * cw:(c + 1) * cw] = _nn(xn, wxbc_ref[:, c * cw:(c + 1) * cw])
    dt_ref[...] = _nn(xn, wdt_ref[...])
    dtT_ref[...] = _nt(wdtT_ref[...], xn)


def _in_proj(x2d, g, w, tabs, tm, tab_period, n_seq=0):
    t = x2d.shape[0]
    nt = t // tm
    if n_seq:
        per_seq = nt // n_seq
        kv_shape = jax.ShapeDtypeStruct((n_seq, D_MODEL, t // n_seq), F32)
        kv_spec = pl.BlockSpec((1, D_MODEL, tm), lambda i: (i // per_seq, 0, i % per_seq))
    else:
        kv_shape = jax.ShapeDtypeStruct((t, D_MODEL), F32)
        kv_spec = pl.BlockSpec((tm, D_MODEL), lambda i: (i, 0))
    full = lambda shape: pl.BlockSpec(shape, lambda i: (0,) * len(shape), pipeline_mode=pl.Buffered(1))
    rows = lambda width: pl.BlockSpec((tm, width), lambda i: (i, 0))
    tab = pl.BlockSpec((tm, LANES), lambda i: (i % tab_period, 0))
    out_shape = (
        jax.ShapeDtypeStruct((t, D_MODEL), BF16), jax.ShapeDtypeStruct((t, D_MODEL), BF16),
        jax.ShapeDtypeStruct((t, D_MODEL), BF16),
        kv_shape, kv_shape,
        jax.ShapeDtypeStruct((t, D_MODEL), F32), jax.ShapeDtypeStruct((t, CONV_DIM), F32),
        jax.ShapeDtypeStruct((t, SSM_HEADS), F32), jax.ShapeDtypeStruct((SSM_HEADS, t), F32),
        jax.ShapeDtypeStruct((nt, 1, D_MODEL), F32),
    )
    out_specs = (rows(D_MODEL), rows(D_MODEL), rows(D_MODEL), kv_spec, kv_spec,
                 rows(D_MODEL), rows(CONV_DIM), rows(SSM_HEADS),
                 pl.BlockSpec((SSM_HEADS, tm), lambda i: (0, i)),
                 pl.BlockSpec((1, 1, D_MODEL), lambda i: (i, 0, 0)))
    return pl.pallas_call(
        functools.partial(_in_proj_kernel, bool(n_seq)),
        out_shape=out_shape,
        grid=(nt,),
        in_specs=[rows(D_MODEL), full((1, D_MODEL)), full(w['wqkv'].shape), full(w['wz'].shape),
                  full(w['wxbc'].shape), full(w['wdt'].shape), full(w['wdtT'].shape), tab, tab, tab],
        out_specs=out_specs,
        compiler_params=pltpu.CompilerParams(dimension_semantics=("arbitrary",), vmem_limit_bytes=VMEM_LIMIT),
        name="in_proj",
    )(x2d, g, w['wqkv'], w['wz'], w['wxbc'], w['wdt'], w['wdtT'], *tabs)


def _rope_tables(pos):
    half = ROT_DIM // 2
    inv_freq = ROPE_THETA ** (-jnp.arange(half, dtype=F32) / half)
    ang = pos.astype(F32)[:, None] * inv_freq[None, :]
    cos, sin = jnp.cos(ang), jnp.sin(ang)
    p = pos.shape[0]
    pad = jnp.zeros((p, HEAD_DIM - ROT_DIM), F32)
    zero = jnp.zeros((p, half), F32)
    cos_t = jnp.concatenate([cos, cos, pad + 1.0], axis=1)
    sina_t = jnp.concatenate([-sin, zero, pad], axis=1)
    sinb_t = jnp.concatenate([zero, sin, pad], axis=1)
    rep = LANES // HEAD_DIM
    return tuple(jnp.tile(a, (1, rep)) for a in (cos_t, sina_t, sinb_t))


def _moba_prompt_kernel(q_ref, k_ref, v_ref, means_ref, g_ref, o_ref, lhs_ref, m_ref, l_ref, acc_ref, out_ref):
    qi = pl.program_id(1)
    tq = ATTN_TQ
    per_blk = MOBA_BLOCK // tq
    qblk = qi // per_blk
    row0 = (qi % per_blk) * tq
    q = q_ref[...]

    means = means_ref[0]
    mrow = lax.broadcasted_iota(jnp.int32, (LANES, D_MODEL), 0)
    mcol = lax.broadcasted_iota(jnp.int32, (LANES, D_MODEL), 1)
    mb = jnp.where(mrow // 8 == mcol // HEAD_DIM, jnp.tile(means, (LANES // 8, 1)), 0.0).astype(BF16)
    gate = _nt(q, mb)
    lane = lax.broadcasted_iota(jnp.int32, (tq, LANES), 1)
    valid = (lane % 8) < qblk
    gate = jnp.where(valid, gate, NEG_INF)
    sel = _group_top3_select(gate, valid)
    selbias = jnp.where(sel, 0.0, NEG_INF)

    krow = lax.broadcasted_iota(jnp.int32, (2 * tq, MOBA_BLOCK), 1)
    qrow = lax.broadcasted_iota(jnp.int32, (2 * tq, MOBA_BLOCK), 0) % tq + row0
    causal = krow <= qrow
    klane = lax.broadcasted_iota(jnp.int32, (MOBA_BLOCK, LANES), 1)
    kstart = pl.multiple_of(qblk * MOBA_BLOCK, MOBA_BLOCK)
    ones = jnp.ones((MOBA_BLOCK, LANES), BF16)
    n_pairs = N_HEADS // 2

    for hp in range(n_pairs):
        cols = slice(hp * LANES, (hp + 1) * LANES)
        qp = q[:, cols].astype(F32)
        rows = []
        for sub in range(2):
            h = 2 * hp + sub
            head_lanes = (lane < HEAD_DIM) if sub == 0 else (lane >= HEAD_DIM)
            bias = selbias if h == 0 else pltpu.roll(selbias, LANES - 8 * h, 1)
            rows.append(jnp.concatenate([jnp.where(head_lanes, qp, 0.0), bias], axis=1))
        lhs = jnp.concatenate(rows, axis=0).astype(BF16)
        lhs_ref[hp] = lhs
        kd = k_ref[pl.ds(kstart, MOBA_BLOCK), cols]
        vd = v_ref[pl.ds(kstart, MOBA_BLOCK), cols]
        s = jnp.where(causal, _nt(lhs[:, :LANES], kd), NEG_INF)
        m0 = jnp.max(s, axis=1, keepdims=True)
        p = jnp.exp(s - m0)
        pv = _nn(p.astype(BF16), jnp.concatenate([vd, ones], axis=1))
        m_ref[hp] = jnp.broadcast_to(m0, (2 * tq, LANES))
        acc_ref[hp] = pv[:, :LANES]
        l_ref[hp] = pv[:, LANES:]

    def body(j, carry):
        ks = pl.multiple_of(j * MOBA_BLOCK, MOBA_BLOCK)
        onehot = jnp.where(klane == j, 1.0, 0.0).astype(BF16)
        for hp in range(n_pairs):
            cols = slice(hp * LANES, (hp + 1) * LANES)
            kb = k_ref[pl.ds(ks, MOBA_BLOCK), cols]
            vb = v_ref[pl.ds(ks, MOBA_BLOCK), cols]
            s = _nt(lhs_ref[hp], jnp.concatenate([kb, onehot], axis=1))
            m_old = m_ref[hp]
            m_new = jnp.maximum(m_old, jnp.max(s, axis=1, keepdims=True))
            alpha = jnp.exp(m_old - m_new)
            p = jnp.exp(s - jnp.concatenate([m_new, m_new], axis=1))
            pv = _nn(p.astype(BF16), jnp.concatenate([vb, ones], axis=1))
            m_ref[hp] = m_new
            acc_ref[hp] = alpha * acc_ref[hp] + pv[:, :LANES]
            l_ref[hp] = alpha * l_ref[hp] + pv[:, LANES:]
        return carry

    lax.fori_loop(0, qblk, body, 0)

    for hp in range(n_pairs):
        o = acc_ref[hp] / l_ref[hp]
        out_ref[:, hp * LANES:(hp + 1) * LANES] = jnp.where(lane < HEAD_DIM, o[:tq], o[tq:])
    a = out_ref[...]
    o_ref[...] = (_rms_scale(a) * g_ref[...]).astype(BF16)


def _moba_prompt(qb, kb, vb, means, g_attn, bn, s):
    nq = s // ATTN_TQ
    nb = means.shape[1]
    return pl.pallas_call(
        _moba_prompt_kernel,
        out_shape=jax.ShapeDtypeStruct((bn * s, D_MODEL), BF16),
        grid=(bn, nq),
        in_specs=[pl.BlockSpec((ATTN_TQ, D_MODEL), lambda b, i: (b * nq + i, 0)),
                  pl.BlockSpec((s, D_MODEL), lambda b, i: (b, 0)),
                  pl.BlockSpec((s, D_MODEL), lambda b, i: (b, 0)),
                  pl.BlockSpec((1, nb, D_MODEL), lambda b, i: (b, 0, 0)),
                  pl.BlockSpec((1, D_MODEL), lambda b, i: (0, 0))],
        out_specs=pl.BlockSpec((ATTN_TQ, D_MODEL), lambda b, i: (b * nq + i, 0)),
        scratch_shapes=[pltpu.VMEM((N_HEADS // 2, 2 * ATTN_TQ, 2 * LANES), BF16),
                        pltpu.VMEM((N_HEADS // 2, 2 * ATTN_TQ, LANES), F32),
                        pltpu.VMEM((N_HEADS // 2, 2 * ATTN_TQ, LANES), F32),
                        pltpu.VMEM((N_HEADS // 2, 2 * ATTN_TQ, LANES), F32),
                        pltpu.VMEM((ATTN_TQ, D_MODEL), F32)],
        compiler_params=pltpu.CompilerParams(dimension_semantics=("arbitrary", "arbitrary"),
                                             vmem_limit_bytes=VMEM_LIMIT),
        name="moba_prompt",
    )(qb, kb, vb, means, g_attn)


def _moba_sample_kernel(pt_ref, q_ref, kn_ref, vn_ref, *refs):
    del pt_ref
    pps = SAMPLE_BLOCKS_PER_STEP * (MOBA_BLOCK // PAGE_SIZE)
    k_pages, v_pages = refs[:pps], refs[pps:2 * pps]
    gcol_ref, o_ref, tok_ref, stat_ref, part_ref, outT_ref = refs[2 * pps:]
    b = pl.program_id(0)
    step = pl.program_id(1)
    db = pl.num_programs(0)
    nsteps = pl.num_programs(1)
    nb = nsteps * SAMPLE_BLOCKS_PER_STEP
    hd3 = (N_HEADS, HEAD_DIM, LANES)

    @pl.when(step == 0)
    def _():
        r = lax.broadcasted_iota(jnp.int32, (q_ref.shape[0], LANES), 0)
        onehot = jnp.where(r == b, 1.0, 0.0).astype(BF16)
        tok_ref[0] = _tn(q_ref[...], onehot)
        tok_ref[1] = _tn(kn_ref[...], onehot)
        tok_ref[2] = _tn(vn_ref[...], onehot)

    @pl.when(jnp.logical_and(step == 0, b == 0))
    def _():
        outT_ref[...] = jnp.zeros(outT_ref.shape, F32)

    q3 = tok_ref[0].reshape(hd3)
    for kk in range(SAMPLE_BLOCKS_PER_STEP):
        n = step * SAMPLE_BLOCKS_PER_STEP + kk
        ka_ref, kb_ref, va_ref, vb_ref = k_pages[2 * kk], k_pages[2 * kk + 1], v_pages[2 * kk], v_pages[2 * kk + 1]
        s_a = jnp.sum(ka_ref[0] * q3, axis=1)
        s_b = jnp.sum(kb_ref[0] * q3, axis=1)
        m = jnp.maximum(jnp.max(s_a, axis=1, keepdims=True), jnp.max(s_b, axis=1, keepdims=True))
        p_a = jnp.exp(s_a - m)
        p_b = jnp.exp(s_b - m)
        l = jnp.sum(p_a, axis=1, keepdims=True) + jnp.sum(p_b, axis=1, keepdims=True)
        gate = (jnp.sum(s_a, axis=1, keepdims=True) + jnp.sum(s_b, axis=1, keepdims=True)) * (1.0 / MOBA_BLOCK)
        part_ref[n] = p_a[:, None, :] * va_ref[0] + p_b[:, None, :] * vb_ref[0]
        stat_ref[n, 0] = jnp.broadcast_to(m, (N_HEADS, LANES))
        stat_ref[n, 1] = jnp.broadcast_to(l, (N_HEADS, LANES))
        stat_ref[n, 2] = jnp.broadcast_to(gate, (N_HEADS, LANES))

    @pl.when(step == nsteps - 1)
    def _():
        lane = lax.broadcasted_iota(jnp.int32, (N_HEADS, LANES), 1)
        gate_all = jnp.zeros((N_HEADS, LANES), F32)
        for j in range(8):
            gate_all = jnp.where(lane == j, stat_ref[j, 2], gate_all)
        valid = lane < nb
        sel = _group_top3_select(jnp.where(valid, gate_all, NEG_INF), valid)
        self3 = jnp.where(lax.broadcasted_iota(jnp.int32, hd3, 2) == 0, tok_ref[2].reshape(hd3), 0.0)
        s_self = jnp.sum(q3 * tok_ref[1].reshape(hd3), axis=1)
        sel_f = jnp.where(sel, 1.0, 0.0)
        sel_j = [jnp.broadcast_to(sel_f[:, j:j + 1], (N_HEADS, LANES)) > 0.5 for j in range(8)]
        mtot = s_self
        for j in range(8):
            mtot = jnp.maximum(mtot, jnp.where(sel_j[j], stat_ref[j, 0], NEG_INF))
        w_self = jnp.exp(s_self - mtot)
        den = w_self
        num = w_self[:, None, :] * self3
        for j in range(8):
            wj = jnp.where(sel_j[j], jnp.exp(stat_ref[j, 0] - mtot), 0.0)
            den = den + wj * stat_ref[j, 1]
            num = num + wj[:, None, :] * part_ref[j]
        out = jnp.broadcast_to(jnp.sum(num, axis=2, keepdims=True), hd3) / den[:, None, :]
        ms = jnp.sum(jnp.sum(out * out, axis=1, keepdims=True), axis=0, keepdims=True) * (1.0 / D_MODEL)
        out = (out * lax.rsqrt(ms + EPS)).reshape(D_MODEL, LANES) * gcol_ref[...]
        col = lax.broadcasted_iota(jnp.int32, (D_MODEL, LANES), 1)
        outT_ref[...] = jnp.where(col == b, out, outT_ref[...])

    @pl.when(jnp.logical_and(step == nsteps - 1, b == db - 1))
    def _():
        o_ref[...] = outT_ref[...].T[:o_ref.shape[0]].astype(BF16)


def _moba_sample(qb, kb_new, vb_new, cache_k, cache_v, page_table, g_attn):
    db = qb.shape[0]
    n_pages = page_table.shape[1]
    ppb = MOBA_BLOCK // PAGE_SIZE
    nb = n_pages // ppb
    pt = page_table.reshape(-1)
    pool_k = jnp.transpose(cache_k, (0, 2, 3, 1))
    pool_v = jnp.transpose(cache_v, (0, 2, 3, 1))
    g_col = jnp.broadcast_to(g_attn.reshape(D_MODEL, 1), (D_MODEL, LANES))
    pps = SAMPLE_BLOCKS_PER_STEP * ppb
    page = lambda k: pl.BlockSpec((1, N_HEADS, HEAD_DIM, PAGE_SIZE),
                                  lambda b, n, pt: (pt[b * n_pages + pps * n + k], 0, 0, 0))
    whole = lambda shape: pl.BlockSpec(shape, lambda b, n, pt: (0,) * len(shape))
    pages = [page(k) for k in range(pps)]
    grid_spec = pltpu.PrefetchScalarGridSpec(
        num_scalar_prefetch=1,
        grid=(db, nb // SAMPLE_BLOCKS_PER_STEP),
        in_specs=[whole((db, D_MODEL)), whole((db, D_MODEL)), whole((db, D_MODEL))] + pages + pages
                 + [whole((D_MODEL, LANES))],
        out_specs=whole((db, D_MODEL)),
        scratch_shapes=[pltpu.VMEM((3, D_MODEL, LANES), F32),
                        pltpu.VMEM((8, 3, N_HEADS, LANES), F32),
                        pltpu.VMEM((8, N_HEADS, HEAD_DIM, LANES), F32),
                        pltpu.VMEM((D_MODEL, LANES), F32)],
    )
    return pl.pallas_call(
        _moba_sample_kernel,
        out_shape=jax.ShapeDtypeStruct((db, D_MODEL), BF16),
        grid_spec=grid_spec,
        compiler_params=pltpu.CompilerParams(dimension_semantics=("arbitrary", "arbitrary"),
                                             vmem_limit_bytes=VMEM_LIMIT),
        name="moba_sample",
    )(pt, qb, kb_new, vb_new, *([pool_k] * pps), *([pool_v] * pps), g_col)


def _ssd_prompt_kernel(xbc_ref, z_ref, dt_ref, dtT_ref, cw_ref, cb_ref, dtb_ref, dtbT_ref, alog_ref,
                       alogT_ref, dskip_ref, gssd_ref, y_ref, conv_ref, ssm_ref, h_ref, xp_ref):
    c = pl.program_id(1)
    q = SSD_CHUNK
    halo = SUBLANES

    @pl.when(c == 0)
    def _():
        h_ref[...] = jnp.zeros(h_ref.shape, F32)
        xp_ref[0:halo, :] = jnp.zeros((halo, CONV_DIM), F32)

    xp_ref[halo:halo + q, :] = xbc_ref[...]
    acc = jnp.broadcast_to(cb_ref[...], (q, CONV_DIM))
    for w in range(CONV_W):
        acc = acc + xp_ref[pl.ds(halo - (CONV_W - 1) + w, q), :] * cw_ref[w:w + 1, :]
    xc = _silu(acc)

    @pl.when(c == pl.num_programs(1) - 1)
    def _():
        conv_ref[0] = xp_ref[halo + q - (CONV_W - 1):halo + q, :]

    xp_ref[0:halo, :] = xp_ref[q:q + halo, :]

    xs = xc[:, :SSM_WIDTH]
    bm = xc[:, SSM_WIDTH:SSM_WIDTH + SSM_GROUPS * D_STATE].astype(BF16)
    cm = xc[:, SSM_WIDTH + SSM_GROUPS * D_STATE:].astype(BF16)

    dt = _softplus(dt_ref[...] + dtb_ref[...])
    dtT = _softplus(dtT_ref[...] + dtbT_ref[...])
    d_a = dt * (-jnp.exp(alog_ref[...]))
    d_aT = dtT * (-jnp.exp(alogT_ref[...]))
    trow = lax.broadcasted_iota(jnp.int32, (q, q), 0)
    tcol = lax.broadcasted_iota(jnp.int32, (q, q), 1)
    tri = tcol <= trow
    tri_b = jnp.where(tri, 1.0, 0.0).astype(BF16)
    acs = sum(_nn(tri_b, part) for part in _split3(d_a))
    acsT = sum(_nt(part, tri_b) for part in _split3(d_aT))
    acs_last = acs[q - 1:q, :]

    lane = lax.broadcasted_iota(jnp.int32, (q, LANES), 1)
    rowi = lax.broadcasted_iota(jnp.int32, (2 * SSM_HEAD_DIM, 1), 0)
    left = lane < SSM_HEAD_DIM
    hg = SSM_HEADS // SSM_GROUPS
    cb_mat = [None] * SSM_GROUPS
    for hp in range(SSM_HEADS // 2):
        g = (2 * hp) // hg
        bg = bm[:, g * D_STATE:(g + 1) * D_STATE]
        cg = cm[:, g * D_STATE:(g + 1) * D_STATE]
        if cb_mat[g] is None:
            cb_mat[g] = _nt(cg, bg)
        cols = slice(hp * LANES, (hp + 1) * LANES)
        x_pair = xs[:, cols]
        x_pair_b = x_pair.astype(BF16)
        h_pair = h_ref[2 * hp:2 * hp + 2].reshape(2 * SSM_HEAD_DIM, D_STATE)
        y_off = _nt(cg, h_pair.astype(BF16))
        y_in, e_col, te_col, dec = [], [], [], []
        for sub in range(2):
            h = 2 * hp + sub
            col = acs[:, h:h + 1]
            seg = col - acsT[h:h + 1, :]
            decay = jnp.where(tri, jnp.exp(jnp.minimum(seg, 0.0)), 0.0)
            wm = (cb_mat[g] * decay * dtT[h:h + 1, :]).astype(BF16)
            y_in.append(_nn(wm, x_pair_b))
            e_col.append(jnp.exp(col))
            te_col.append(jnp.exp(acs_last[:, h:h + 1] - col) * dt[:, h:h + 1])
            dec.append(jnp.exp(acs_last[:, h:h + 1]))
        y_pair = jnp.where(left, y_in[0], y_in[1]) + y_off * jnp.where(left, e_col[0], e_col[1])
        xw = (x_pair * jnp.where(left, te_col[0], te_col[1])).astype(BF16)
        upd = _tn(xw, bg)
        h_new = h_pair * jnp.where(rowi < SSM_HEAD_DIM, dec[0], dec[1]) + upd
        h_ref[2 * hp:2 * hp + 2] = h_new.reshape(2, SSM_HEAD_DIM, D_STATE)
        yp = y_pair + dskip_ref[:, cols] * x_pair
        zp = z_ref[:, cols]
        xp_ref[halo:halo + q, cols] = yp * _silu(zp)

    gw = SSM_WIDTH // SSM_GROUPS
    for g in range(SSM_GROUPS):
        yg = xp_ref[halo:halo + q, g * gw:(g + 1) * gw]
        y_ref[:, g * gw:(g + 1) * gw] = (_rms_scale(yg) * gssd_ref[:, g * gw:(g + 1) * gw]).astype(BF16)

    @pl.when(c == pl.num_programs(1) - 1)
    def _():
        ssm_ref[0] = h_ref[...]


def _ssd_prompt(xbc, z, dt, dtT, p, bn, s):
    nc = s // SSD_CHUNK
    q = SSD_CHUNK
    const = lambda shape: pl.BlockSpec(shape, lambda b, c: (0,) * len(shape))
    rows = lambda width: pl.BlockSpec((q, width), lambda b, c: (b * nc + c, 0))
    return pl.pallas_call(
        _ssd_prompt_kernel,
        out_shape=(jax.ShapeDtypeStruct((bn * s, SSM_WIDTH), BF16),
                   jax.ShapeDtypeStruct((bn, CONV_W - 1, CONV_DIM), F32),
                   jax.ShapeDtypeStruct((bn, SSM_HEADS, SSM_HEAD_DIM, D_STATE), F32)),
        grid=(bn, nc),
        in_specs=[rows(CONV_DIM), rows(SSM_WIDTH), rows(SSM_HEADS),
                  pl.BlockSpec((SSM_HEADS, q), lambda b, c: (0, b * nc + c)),
                  const((CONV_W, CONV_DIM)), const((1, CONV_DIM)), const((1, SSM_HEADS)),
                  const((SSM_HEADS, 1)), const((1, SSM_HEADS)), const((SSM_HEADS, 1)),
                  const((1, SSM_WIDTH)), const((1, SSM_WIDTH))],
        out_specs=(rows(SSM_WIDTH),
                   pl.BlockSpec((1, CONV_W - 1, CONV_DIM), lambda b, c: (b, 0, 0)),
                   pl.BlockSpec((1, SSM_HEADS, SSM_HEAD_DIM, D_STATE), lambda b, c: (b, 0, 0, 0))),
        scratch_shapes=[pltpu.VMEM((SSM_HEADS, SSM_HEAD_DIM, D_STATE), F32),
                        pltpu.VMEM((SUBLANES + q, CONV_DIM), F32)],
        compiler_params=pltpu.CompilerParams(dimension_semantics=("arbitrary", "arbitrary"),
                                             vmem_limit_bytes=VMEM_LIMIT),
        name="ssd_prompt",
    )(xbc, z, dt, dtT, p['conv_w'], p['conv_b'], p['dt_bias'], p['dt_biasT'], p['a_log'], p['a_logT'],
      p['d_skip'], p['g_ssd'])


def _ssd_sample_pre_kernel(sc_ref, xbc_ref, dt_ref, cw_ref, cb_ref, dtb_ref, alog_ref, ex_ref,
                           conv_ref, xs_ref, b_ref, c_ref, dtxT_ref, decT_ref):
    acc = jnp.broadcast_to(cb_ref[...], xbc_ref.shape)
    for w in range(CONV_W - 1):
        acc = acc + sc_ref[:, w * CONV_DIM:(w + 1) * CONV_DIM] * cw_ref[w:w + 1, :]
    xnew = xbc_ref[...]
    acc = acc + xnew * cw_ref[CONV_W - 1:CONV_W, :]
    xc = _silu(acc)
    conv_ref[:, 0:(CONV_W - 2) * CONV_DIM] = sc_ref[:, CONV_DIM:(CONV_W - 1) * CONV_DIM]
    conv_ref[:, (CONV_W - 2) * CONV_DIM:] = xnew
    xs = xc[:, :SSM_WIDTH]
    xs_ref[...] = xs
    b_ref[...] = xc[:, SSM_WIDTH:SSM_WIDTH + SSM_GROUPS * D_STATE]
    c_ref[...] = xc[:, SSM_WIDTH + SSM_GROUPS * D_STATE:]
    dt = _softplus(dt_ref[...] + dtb_ref[...])
    d_a = dt * (-jnp.exp(alog_ref[...]))
    ex = ex_ref[...]
    dt_l = sum(_nn(part, ex) for part in _split3(dt))
    da_l = sum(_nn(part, ex) for part in _split3(d_a))
    dtxT_ref[...] = (dt_l * xs).T
    decT_ref[...] = jnp.exp(da_l).T


def _ssd_sample_step_kernel(dtxT_ref, decT_ref, b_ref, c_ref, h_ref, hn_ref, y_ref):
    db = dtxT_ref.shape[1]
    base = pl.multiple_of(pl.program_id(0) * SUBLANES, SUBLANES)
    b_slab = b_ref[pl.ds(base, SUBLANES), :]
    c_slab = c_ref[pl.ds(base, SUBLANES), :]
    dtx_parts = _split3(dtxT_ref[...])
    dec_parts = _split3(decT_ref[...])
    r = lax.broadcasted_iota(jnp.int32, (db, LANES), 0)
    gw = SSM_WIDTH // SSM_GROUPS
    hg_n = SSM_HEADS // SSM_GROUPS
    for kk in range(SUBLANES):
        onehot = jnp.where(r == base + kk, 1.0, 0.0).astype(BF16)
        dtx_bc = sum(_nn(part, onehot) for part in dtx_parts)
        dec_bc = sum(_nn(part, onehot) for part in dec_parts)
        rows = []
        for g in range(SSM_GROUPS):
            b_row = b_slab[kk:kk + 1, g * D_STATE:(g + 1) * D_STATE]
            c_row = c_slab[kk:kk + 1, g * D_STATE:(g + 1) * D_STATE].astype(BF16)
            hg = h_ref[kk, g * hg_n:(g + 1) * hg_n].reshape(gw, D_STATE)
            h_new = hg * dec_bc[g * gw:(g + 1) * gw] + dtx_bc[g * gw:(g + 1) * gw] * b_row
            hn_ref[kk, g * hg_n:(g + 1) * hg_n] = h_new.reshape(hg_n, SSM_HEAD_DIM, D_STATE)
            rows.append(_nt(jnp.broadcast_to(c_row, (SUBLANES, D_STATE)), h_new.astype(BF16))[0:1])
        y_ref[kk] = jnp.concatenate(rows, axis=1)


def _ssd_sample_post_kernel(y_ref, xs_ref, z_ref, dskip_ref, gssd_ref, o_ref):
    y = (y_ref[...] + dskip_ref[...] * xs_ref[...]) * _silu(z_ref[...])
    gw = SSM_WIDTH // SSM_GROUPS
    for g in range(SSM_GROUPS):
        yg = y[:, g * gw:(g + 1) * gw]
        o_ref[:, g * gw:(g + 1) * gw] = (_rms_scale(yg) * gssd_ref[:, g * gw:(g + 1) * gw]).astype(BF16)


def _whole(shape):
    return pl.BlockSpec(shape, lambda *_: (0,) * len(shape))


def _ssd_sample(state_conv, xbc, z, dt, state_ssm, p):
    db = xbc.shape[0]
    sc2 = state_conv.reshape(db, (CONV_W - 1) * CONV_DIM)
    ex = (jnp.arange(SSM_WIDTH)[None, :] // SSM_HEAD_DIM == jnp.arange(SSM_HEADS)[:, None]).astype(BF16)
    pre_in = (sc2, xbc, dt, p['conv_w'], p['conv_b'], p['dt_bias'], p['a_log'], ex)
    pre_out = (jax.ShapeDtypeStruct(sc2.shape, F32), jax.ShapeDtypeStruct((db, SSM_WIDTH), F32),
               jax.ShapeDtypeStruct((db, SSM_GROUPS * D_STATE), F32),
               jax.ShapeDtypeStruct((db, SSM_GROUPS * D_STATE), F32),
               jax.ShapeDtypeStruct((SSM_WIDTH, db), F32), jax.ShapeDtypeStruct((SSM_WIDTH, db), F32))
    conv_new, xs, bm, cm, dtxT, decT = pl.pallas_call(
        _ssd_sample_pre_kernel, out_shape=pre_out, grid=(1,),
        in_specs=[_whole(a.shape) for a in pre_in], out_specs=tuple(_whole(o.shape) for o in pre_out),
        compiler_params=pltpu.CompilerParams(vmem_limit_bytes=VMEM_LIMIT), name="ssd_sample_pre",
    )(*pre_in)
    st = pl.BlockSpec((SUBLANES, SSM_HEADS, SSM_HEAD_DIM, D_STATE), lambda b: (b, 0, 0, 0))
    ssm_new, y = pl.pallas_call(
        _ssd_sample_step_kernel,
        out_shape=(jax.ShapeDtypeStruct(state_ssm.shape, F32), jax.ShapeDtypeStruct((db, 1, SSM_WIDTH), F32)),
        grid=(db // SUBLANES,),
        in_specs=[_whole(dtxT.shape), _whole(decT.shape), _whole(bm.shape), _whole(cm.shape), st],
        out_specs=(st, pl.BlockSpec((SUBLANES, 1, SSM_WIDTH), lambda b: (b, 0, 0))),
        compiler_params=pltpu.CompilerParams(dimension_semantics=("arbitrary",), vmem_limit_bytes=VMEM_LIMIT),
        name="ssd_sample_step",
    )(dtxT, decT, bm, cm, state_ssm)
    post_in = (y.reshape(db, SSM_WIDTH), xs, z, p['d_skip'], p['g_ssd'])
    y_n = pl.pallas_call(
        _ssd_sample_post_kernel, out_shape=jax.ShapeDtypeStruct((db, SSM_WIDTH), BF16), grid=(1,),
        in_specs=[_whole(a.shape) for a in post_in], out_specs=_whole((db, SSM_WIDTH)),
        name="ssd_sample_post",
    )(*post_in)
    return y_n, conv_new.reshape(db, CONV_W - 1, CONV_DIM), ssm_new


def _merge_exchange_pairs(n):
    t = n.bit_length() - 1
    pairs = []
    p = 1 << (t - 1)
    while p > 0:
        q, r, d = 1 << (t - 1), 0, p
        while d > 0:
            for i in range(n - d):
                if (i & p) == r:
                    pairs.append((i, i + d))
            d, q, r = q - p, q >> 1, p
        p >>= 1
    return pairs


_SORT16 = _merge_exchange_pairs(PEER_TOPK)


def _sort_desc(v):
    v = list(v)
    for i, j in _SORT16:
        hi, lo = jnp.maximum(v[i], v[j]), jnp.minimum(v[i], v[j])
        v[i], v[j] = hi, lo
    return v


def _bitonic_merge_desc(v):
    v = list(v)
    stride = len(v) // 2
    while stride >= 1:
        for i in range(len(v)):
            if (i & stride) == 0:
                hi, lo = jnp.maximum(v[i], v[i + stride]), jnp.minimum(v[i], v[i + stride])
                v[i], v[i + stride] = hi, lo
        stride //= 2
    return v


def _top_merge(a, b):
    k = len(a)
    return [jnp.maximum(a[i], b[k - 1 - i]) for i in range(k)]


def _top16_desc(s):
    v = _sort_desc([s[SUBLANES * r:SUBLANES * (r + 1), :] for r in range(PEER_NKEYS // SUBLANES)])
    for shift in (4, 2, 1):
        v = _bitonic_merge_desc(_top_merge(v, [pltpu.roll(x, shift, 0) for x in v]))
    return v


def _peer_front_kernel(x_ref, attn_ref, yssd_ref, wout_ref, gffn_ref, wq_ref, keys_ref,
                       h1_ref, xn_ref, r1_ref, e1_ref, k_ref, e0_ref, s_ref):
    tm = x_ref.shape[0]
    h1 = (x_ref[...] + _nn(attn_ref[...], wout_ref[0:D_MODEL, :])
          + _nn(yssd_ref[...], wout_ref[D_MODEL:2 * D_MODEL, :]))
    h1_ref[...] = h1
    xn = (_rms_scale(h1) * gffn_ref[...]).astype(BF16)
    xn_ref[...] = xn
    sub = lax.broadcasted_iota(jnp.int32, (SUBLANES, tm), 0)
    packed = [[None] * PEER_TOPK, [None] * PEER_TOPK]
    for hh in range(2 * PEER_HEADS):
        h, half = hh // 2, hh % 2
        qh = _nn(xn, wq_ref[:, hh * PEER_NKEYS:(hh + 1) * PEER_NKEYS]).astype(BF16)
        st = _nt(keys_ref[hh], qh)
        s_ref[hh] = st
        top = _top16_desc(st)
        for a in range(PEER_TOPK):
            packed[half][a] = top[a] if h == 0 else jnp.where(sub == h, top[a], packed[half][a])
    p0, p1 = packed
    cand = lambda a, b: p0[a] + p1[b]
    g1 = [cand(0, b) for b in range(16)]
    g2 = _bitonic_merge_desc([cand(1, b) for b in range(8)] + [cand(a, 0) for a in range(15, 7, -1)])
    rest = [cand(a, b) for a in range(2, 8) for b in range(PEER_TOPK // (a + 1))]
    g3 = _sort_desc(rest[:16])
    x_hi, x_lo = jnp.maximum(rest[16], rest[17]), jnp.minimum(rest[16], rest[17])
    g3[14] = jnp.maximum(g3[14], x_lo)
    g3[15] = jnp.maximum(g3[15], x_hi)
    m12 = _bitonic_merge_desc(_top_merge(g1, g2))
    m34 = _bitonic_merge_desc(g3)
    best = _top_merge(m12, m34)
    tau = functools.reduce(jnp.minimum, best)
    top = g1[0]
    zsum = functools.reduce(lambda u, w: u + w, [jnp.exp(v - top) for v in best])
    rz = 1.0 / zsum
    for h in range(PEER_HEADS):
        s0 = s_ref[2 * h]
        s1 = s_ref[2 * h + 1]
        tau_h = tau[h:h + 1, :]
        k = jnp.zeros(s0.shape, F32)
        for b in range(PEER_TOPK):
            k = jnp.where((p1[b][h:h + 1, :] + s0) >= tau_h, float(b + 1), k)
        rank = jnp.full(s1.shape, float(PEER_TOPK), F32)
        for b in range(PEER_TOPK - 1, -1, -1):
            rank = jnp.where(p1[b][h:h + 1, :] <= s1, float(b), rank)
        k_ref[h] = k
        r1_ref[h] = rank.astype(BF16)
        e0_ref[h] = jnp.exp(s0 - p0[0][h:h + 1, :]) * rz[h:h + 1, :]
        e1_ref[h] = jnp.exp(s1 - p1[0][h:h + 1, :]).astype(BF16)


def _peer_front(x2d, attn_n, y_n, w, tm):
    t = x2d.shape[0]
    nt = t // tm
    rows = lambda width: pl.BlockSpec((tm, width), lambda i: (i, 0))
    full = lambda shape: pl.BlockSpec(shape, lambda i: (0,) * len(shape), pipeline_mode=pl.Buffered(1))
    per_head = lambda dt: jax.ShapeDtypeStruct((PEER_HEADS, PEER_NKEYS, t), dt)
    tile = pl.BlockSpec((PEER_HEADS, PEER_NKEYS, tm), lambda i: (0, 0, i))
    return pl.pallas_call(
        _peer_front_kernel,
        out_shape=(jax.ShapeDtypeStruct((t, D_MODEL), F32), jax.ShapeDtypeStruct((t, D_MODEL), BF16),
                   per_head(BF16), per_head(BF16), per_head(F32), per_head(F32)),
        grid=(nt,),
        in_specs=[rows(D_MODEL), rows(D_MODEL), rows(D_MODEL), full(w['w_out'].shape), full((1, D_MODEL)),
                  full(w['wq'].shape), full(w['keys'].shape)],
        out_specs=(rows(D_MODEL), rows(D_MODEL), tile, tile, tile, tile),
        scratch_shapes=[pltpu.VMEM((2 * PEER_HEADS, PEER_NKEYS, tm), F32)],
        compiler_params=pltpu.CompilerParams(dimension_semantics=("arbitrary",), vmem_limit_bytes=VMEM_LIMIT),
        name="peer_front",
    )(x2d, attn_n, y_n, w['w_out'], w['g_ffn'], w['wq'], w['keys'])


def _peer_dense_kernel(xn_ref, u_ref, vt_ref, r1_ref, e1_ref, k_ref, e0_ref, h1_ref, gfin_ref, y_ref,
                       acc_ref, wt_ref):
    ec = pl.program_id(1)

    @pl.when(ec == 0)
    def _():
        acc_ref[...] = jnp.zeros(acc_ref.shape, F32)

    base = pl.multiple_of(ec * PEER_IPS, PEER_IPS)
    k_rows = [k_ref[h, pl.ds(base, PEER_IPS), :] for h in range(PEER_HEADS)]
    e0_rows = [e0_ref[h, pl.ds(base, PEER_IPS), :] for h in range(PEER_HEADS)]
    xn = xn_ref[...]
    tt = xn.shape[0]
    cw = min(2 * LANES, tt)
    zero = jnp.zeros((), BF16)
    for ii in range(PEER_IPS):
        rows = slice(ii * PEER_NKEYS, (ii + 1) * PEER_NKEYS)
        act = _nt(u_ref[rows, :], xn)
        act = (0.5 * act * (1.0 + lax.erf(act * (2.0 ** -0.5)))).astype(BF16)
        for sub in range(tt // cw):
            ls = slice(sub * cw, (sub + 1) * cw)
            g = None
            for h in range(PEER_HEADS):
                kb = jnp.broadcast_to(k_rows[h][ii:ii + 1, ls], (PEER_NKEYS, cw)).astype(BF16)
                eb = jnp.broadcast_to(e0_rows[h][ii:ii + 1, ls], (PEER_NKEYS, cw)).astype(BF16)
                w = jnp.where(r1_ref[h, :, ls] < kb, e1_ref[h, :, ls], zero) * eb
                g = w if g is None else g + w
            wt_ref[rows, ls] = g * act[:, ls]
    acc_ref[...] += _nn(vt_ref[...], wt_ref[...])

    @pl.when(ec == pl.num_programs(1) - 1)
    def _():
        hf = h1_ref[...] + acc_ref[...].T
        y_ref[...] = _rms_scale(hf) * gfin_ref[...]


def _peer_dense(xn, u_b, vt_b, r1, e1, kk, e0, h1, g_final, tt):
    t = xn.shape[0]
    n_exp = u_b.shape[0]
    tile = pl.BlockSpec((PEER_HEADS, PEER_NKEYS, tt), lambda i, e: (0, 0, i))
    return pl.pallas_call(
        _peer_dense_kernel,
        out_shape=jax.ShapeDtypeStruct((t, D_MODEL), F32),
        grid=(t // tt, n_exp // PEER_EC),
        in_specs=[pl.BlockSpec((tt, D_MODEL), lambda i, e: (i, 0)),
                  pl.BlockSpec((PEER_EC, D_MODEL), lambda i, e: (e, 0)),
                  pl.BlockSpec((D_MODEL, PEER_EC), lambda i, e: (0, e)),
                  tile, tile, tile, tile,
                  pl.BlockSpec((tt, D_MODEL), lambda i, e: (i, 0)),
                  pl.BlockSpec((1, D_MODEL), lambda i, e: (0, 0))],
        out_specs=pl.BlockSpec((tt, D_MODEL), lambda i, e: (i, 0)),
        scratch_shapes=[pltpu.VMEM((D_MODEL, tt), F32), pltpu.VMEM((PEER_EC, tt), BF16)],
        compiler_params=pltpu.CompilerParams(dimension_semantics=("arbitrary", "arbitrary"),
                                             vmem_limit_bytes=VMEM_LIMIT),
        name="peer_dense",
    )(xn, u_b, vt_b, r1, e1, kk, e0, h1, g_final)


def kernel(x_prompt, x_sample, cache_k, cache_v, page_table, state_conv, state_ssm, g_mix, w_in, conv_w, conv_b, dt_bias, a_log, d_skip, g_attn, g_ssd, w_out, g_ffn, peer_wq, peer_keys, peer_u, peer_v, g_final):
    bn, s, d = x_prompt.shape
    db, t_new, _ = x_sample.shape
    n_pages = page_table.shape[1]
    assert d == D_MODEL and w_in.shape[0] == 1 and t_new == 1
    assert s % MOBA_BLOCK == 0 and s // MOBA_BLOCK <= 8 and s % SSD_CHUNK == 0
    assert n_pages * PAGE_SIZE == 8 * MOBA_BLOCK and db % LANES == 0
    past = n_pages * PAGE_SIZE

    wi = w_in[0]
    qkv_w, z_w = 3 * D_MODEL, SSM_WIDTH
    w = {
        'wqkv': wi[:, :qkv_w].astype(BF16),
        'wz': wi[:, qkv_w:qkv_w + z_w].astype(BF16),
        'wxbc': wi[:, qkv_w + z_w:qkv_w + z_w + CONV_DIM].astype(BF16),
        'wdt': wi[:, qkv_w + z_w + CONV_DIM:].astype(BF16),
        'wdtT': wi[:, qkv_w + z_w + CONV_DIM:].T.astype(BF16),
        'w_out': w_out[0].astype(BF16),
        'g_ffn': g_ffn[0].reshape(1, D_MODEL),
        'wq': peer_wq[0].astype(BF16),
        'keys': peer_keys[0].reshape(2 * PEER_HEADS, PEER_NKEYS, PEER_NKEYS).astype(BF16),
    }
    p = {
        'conv_w': conv_w[0], 'conv_b': conv_b[0].reshape(1, CONV_DIM),
        'dt_bias': dt_bias[0].reshape(1, SSM_HEADS), 'dt_biasT': dt_bias[0].reshape(SSM_HEADS, 1),
        'a_log': a_log[0].reshape(1, SSM_HEADS), 'a_logT': a_log[0].reshape(SSM_HEADS, 1),
        'd_skip': jnp.repeat(d_skip[0], SSM_HEAD_DIM).reshape(1, SSM_WIDTH),
        'g_ssd': g_ssd[0].reshape(1, SSM_WIDTH),
    }
    g_mix2 = g_mix[0].reshape(1, D_MODEL)
    g_attn2 = g_attn[0].reshape(1, D_MODEL)
    g_fin2 = g_final.reshape(1, D_MODEL)
    u_b = peer_u[0].astype(BF16)
    vt_b = peer_v[0].T.astype(BF16)

    xp2 = x_prompt.reshape(bn * s, D_MODEL)
    tabs_p = _rope_tables(jnp.arange(s, dtype=jnp.int32))
    qb, kb, vb, k_p, v_p, z_p, xbc_p, dt_p, dtT_p, kmean = _in_proj(
        xp2, g_mix2, w, tabs_p, MOBA_BLOCK, s // MOBA_BLOCK, n_seq=bn)
    nb = s // MOBA_BLOCK
    means = jnp.pad(kmean.reshape(bn, nb, D_MODEL), ((0, 0), (0, 8 - nb), (0, 0)))
    attn_p = _moba_prompt(qb, kb, vb, means, g_attn2, bn, s)
    y_p, conv_p, ssm_p = _ssd_prompt(xbc_p, z_p, dt_p, dtT_p, p, bn, s)
    h1_p, xn_p, *gate_p = _peer_front(xp2, attn_p, y_p, w, MOBA_BLOCK)
    y_prompt = _peer_dense(xn_p, u_b, vt_b, *gate_p, h1_p, g_fin2, PEER_TT)

    xs2 = x_sample.reshape(db, D_MODEL)
    tabs_s = _rope_tables(jnp.full((LANES,), past, dtype=jnp.int32))
    qs, ks, vs, k_s, v_s, z_s, xbc_s, dt_s, _, _ = _in_proj(xs2, g_mix2, w, tabs_s, LANES, 1)
    attn_s = _moba_sample(qs, ks, vs, cache_k[0], cache_v[0], page_table, g_attn2)
    y_s, conv_s, ssm_s = _ssd_sample(state_conv[0], xbc_s, z_s, dt_s, state_ssm[0], p)
    h1_s, xn_s, *gate_s = _peer_front(xs2, attn_s, y_s, w, LANES)
    y_sample = _peer_dense(xn_s, u_b, vt_b, *gate_s, h1_s, g_fin2, LANES)

    heads = (N_HEADS, HEAD_DIM)
    return (y_prompt.reshape(bn, s, D_MODEL), y_sample.reshape(db, 1, D_MODEL),
            jnp.transpose(k_p.reshape((bn,) + heads + (s,)), (0, 3, 1, 2))[None],
            jnp.transpose(v_p.reshape((bn,) + heads + (s,)), (0, 3, 1, 2))[None],
            k_s.reshape((1, db, 1) + heads), v_s.reshape((1, db, 1) + heads),
            conv_p[None], conv_s[None], ssm_p[None], ssm_s[None])
```
